```python
import math
import jax, jax.numpy as jnp
from jax import lax
import numpy as np

D_MODEL = 1024
BATCH = 8
SEQ = 4096
DEPTH = 2

HEAD_DIM = 64
SB_HEADS = 8
SB_WIDTH = SB_HEADS * HEAD_DIM
LRU_WIDTH = 512
LRU_BLOCKS = 8
LRU_BLOCK = LRU_WIDTH // LRU_BLOCKS
CONV_WIDTH = 4
LRU_C = 8.0
DIFF_HEADS = 8
DIFF_VDIM = 2 * HEAD_DIM
DIFF_QK = 2 * DIFF_HEADS * HEAD_DIM
DIFF_V = DIFF_HEADS * DIFF_VDIM
EVEN_IN = 3 * SB_WIDTH + 2 * LRU_WIDTH
ODD_IN = 2 * DIFF_QK + DIFF_V
N_BUCKETS = 32
MAX_EXACT = N_BUCKETS // 2
MAX_DISTANCE = 128
D_FF_DENSE = 2816
N_EXPERTS = 8
TOP_K = 2
D_FF_EXPERT = 3584
Q_BLOCK = 128
EPS = 1e-6

kernel_name = "hybrid_sb_rglru_diffattn_moe_adaln"


def rmsnorm(x, g):
    xf = x.astype(jnp.float32)
    n = xf * lax.rsqrt(jnp.mean(xf * xf, axis=-1, keepdims=True) + EPS)
    return (n * g.astype(jnp.float32)).astype(x.dtype)


def modulate(h, shift, scale):
    return h * (1.0 + scale[:, None, :]) + shift[:, None, :]


def _query_blocks(q):
    b, h, s, d = q.shape
    return jnp.moveaxis(q.reshape(b, h, s // Q_BLOCK, Q_BLOCK, d), 2, 0)


def _merge_blocks(o):
    nb, b, h, qb, d = o.shape
    return jnp.moveaxis(o, 0, 2).reshape(b, h, nb * qb, d)


def stick_breaking_attention(q, k, v):
    s_len = q.shape[2]
    scale = HEAD_DIM ** -0.5
    key_pos = jnp.arange(s_len, dtype=jnp.int32)

    def block(args):
        qb, blk = args
        z = jnp.einsum('bhqd,bhkd->bhqk', qb, k).astype(jnp.float32) * scale
        q_pos = blk * Q_BLOCK + jnp.arange(Q_BLOCK, dtype=jnp.int32)
        mask = key_pos[None, :] < q_pos[:, None]
        log_keep = jnp.where(mask, jax.nn.log_sigmoid(-z), 0.0)
        later = lax.cumsum(log_keep, axis=3, reverse=True) - log_keep
        w = jnp.where(mask, jnp.exp(jax.nn.log_sigmoid(z) + later), 0.0)
        return jnp.einsum('bhqk,bhkd->bhqd', w.astype(v.dtype), v)

    nb = s_len // Q_BLOCK
    out = lax.map(block, (_query_blocks(q), jnp.arange(nb, dtype=jnp.int32)))
    return _merge_blocks(out)


def rg_lru_branch(xb, gb, conv_w, conv_b, ga_w, ga_b, gx_w, gx_b, lam):
    b, s_len, c = xb.shape
    xc = lax.conv_general_dilated(
        xb, conv_w[:, None, :], window_strides=(1,), padding=[(CONV_WIDTH - 1, 0)],
        dimension_numbers=('NWC', 'WIO', 'NWC'), feature_group_count=c) + conv_b
    xg = xc.reshape(b, s_len, LRU_BLOCKS, LRU_BLOCK)
    r = jax.nn.sigmoid(jnp.einsum('bsgi,gij->bsgj', xg, ga_w).reshape(b, s_len, c) + ga_b)
    i = jax.nn.sigmoid(jnp.einsum('bsgi,gij->bsgj', xg, gx_w).reshape(b, s_len, c) + gx_b)
    log_a = LRU_C * r.astype(jnp.float32) * jax.nn.log_sigmoid(lam.astype(jnp.float32))
    a = jnp.exp(log_a)
    u = jnp.sqrt(-jnp.expm1(2.0 * log_a)) * (i * xc).astype(jnp.float32)

    def combine(left, right):
        a1, b1 = left
        a2, b2 = right
        return a1 * a2, a2 * b1 + b2

    _, h = lax.associative_scan(combine, (a, u), axis=1)
    return h.astype(xb.dtype) * jax.nn.gelu(gb)


def even_mixer(h, w_in, w_out, conv_w, conv_b, ga_w, ga_b, gx_w, gx_b, lam):
    b, s_len, _ = h.shape
    proj = h @ w_in
    q, k, v, xb, gb = jnp.split(
        proj, [SB_WIDTH, 2 * SB_WIDTH, 3 * SB_WIDTH, 3 * SB_WIDTH + LRU_WIDTH], axis=-1)
    heads = lambda t: t.reshape(b, s_len, SB_HEADS, HEAD_DIM).transpose(0, 2, 1, 3)
    ya = stick_breaking_attention(heads(q), heads(k), heads(v))
    ya = ya.transpose(0, 2, 1, 3).reshape(b, s_len, SB_WIDTH)
    yb = rg_lru_branch(xb, gb, conv_w, conv_b, ga_w, ga_b, gx_w, gx_b, lam)
    return jnp.concatenate([ya, yb], axis=-1) @ w_out


def t5_bucket(rel):
    n = jnp.maximum(rel, 0)
    nf = jnp.maximum(n, 1).astype(jnp.float32)
    large = MAX_EXACT + (jnp.log(nf / MAX_EXACT) / math.log(MAX_DISTANCE / MAX_EXACT)
                         * (N_BUCKETS - MAX_EXACT)).astype(jnp.int32)
    large = jnp.minimum(large, N_BUCKETS - 1)
    return jnp.where(n < MAX_EXACT, n, large)


def differential_attention(q, k, v, rel_bias, lam):
    b, _, s_len, _ = q.shape
    scale = HEAD_DIM ** -0.5
    key_pos = jnp.arange(s_len, dtype=jnp.int32)

    def block(args):
        qb, blk = args
        z = jnp.einsum('bhqd,bhkd->bhqk', qb, k).astype(jnp.float32) * scale
        z = z.reshape(b, DIFF_HEADS, 2, Q_BLOCK, s_len)
        q_pos = blk * Q_BLOCK + jnp.arange(Q_BLOCK, dtype=jnp.int32)
        rel = q_pos[:, None] - key_pos[None, :]
        bias = jnp.transpose(rel_bias[t5_bucket(rel)].astype(jnp.float32), (2, 0, 1))
        z = jnp.where(rel >= 0, z + bias[None, :, None], -jnp.inf)
        p = jax.nn.softmax(z, axis=-1)
        attn = p[:, :, 0] - lam * p[:, :, 1]
        return jnp.einsum('bhqk,bhkd->bhqd', attn.astype(v.dtype), v)

    nb = s_len // Q_BLOCK
    out = lax.map(block, (_query_blocks(q), jnp.arange(nb, dtype=jnp.int32)))
    return _merge_blocks(out)


def odd_mixer(h, w_in, w_out, rel_bias, lq1, lk1, lq2, lk2, subln, lambda_init):
    b, s_len, _ = h.shape
    proj = h @ w_in
    q, k, v = jnp.split(proj, [DIFF_QK, 2 * DIFF_QK], axis=-1)
    q = q.reshape(b, s_len, 2 * DIFF_HEADS, HEAD_DIM).transpose(0, 2, 1, 3)
    k = k.reshape(b, s_len, 2 * DIFF_HEADS, HEAD_DIM).transpose(0, 2, 1, 3)
    v = v.reshape(b, s_len, DIFF_HEADS, DIFF_VDIM).transpose(0, 2, 1, 3)
    lam = (jnp.exp(jnp.sum(lq1.astype(jnp.float32) * lk1.astype(jnp.float32)))
           - jnp.exp(jnp.sum(lq2.astype(jnp.float32) * lk2.astype(jnp.float32)))
           + lambda_init)
    o = differential_attention(q, k, v, rel_bias, lam)
    o = rmsnorm(o, subln) * (1.0 - lambda_init)
    return o.transpose(0, 2, 1, 3).reshape(b, s_len, DIFF_V) @ w_out


def swiglu(h, w_gate, w_up, w_down):
    return (jax.nn.silu(h @ w_gate) * (h @ w_up)) @ w_down


def moe_ffn(h, router_w, router_b, wg, wu, wd):
    logits = (jnp.einsum('bsd,de->bse', h, router_w) + router_b).astype(jnp.float32)
    top_vals, top_idx = lax.top_k(logits, TOP_K)
    top_w = jax.nn.softmax(top_vals, axis=-1)
    gates = jnp.einsum('bsk,bske->bse', top_w,
                       jax.nn.one_hot(top_idx, N_EXPERTS, dtype=jnp.float32))
    y = jnp.zeros_like(h)
    for e in range(N_EXPERTS):
        y = y + gates[..., e:e + 1].astype(h.dtype) * swiglu(h, wg[e], wu[e], wd[e])
    return y


def setup_inputs(seed: int = 0) -> dict:
    key = jax.random.key(seed)
    ks = jax.random.split(key, 32)
    f32 = jnp.float32
    ne = (DEPTH + 1) // 2
    no = DEPTH // 2
    nrm = lambda k, shape, s: jax.random.normal(k, shape, f32) * s
    u = jax.random.uniform(ks[16], (ne, LRU_WIDTH), f32, 0.9, 0.999)
    root = u ** (1.0 / LRU_C)
    return {
        "x": nrm(ks[0], (BATCH, SEQ, D_MODEL), 1.0),
        "c": nrm(ks[1], (BATCH, D_MODEL), 1.0),
        "rel_bias": nrm(ks[2], (N_BUCKETS, DIFF_HEADS), 0.5),
        "ada_w": nrm(ks[3], (DEPTH, D_MODEL, 6 * D_MODEL), D_MODEL ** -0.5),
        "ada_b": nrm(ks[4], (DEPTH, 6 * D_MODEL), 0.02),
        "ln_mix": 1.0 + nrm(ks[5], (DEPTH, D_MODEL), 0.1),
        "ln_ffn": 1.0 + nrm(ks[6], (DEPTH, D_MODEL), 0.1),
        "ln_final": 1.0 + nrm(ks[7], (D_MODEL,), 0.1),
        "even_w_in": nrm(ks[8], (ne, D_MODEL, EVEN_IN), D_MODEL ** -0.5),
        "even_w_out": nrm(ks[9], (ne, SB_WIDTH + LRU_WIDTH, D_MODEL), (SB_WIDTH + LRU_WIDTH) ** -0.5),
        "lru_conv_w": nrm(ks[10], (ne, CONV_WIDTH, LRU_WIDTH), CONV_WIDTH ** -0.5),
        "lru_conv_b": nrm(ks[11], (ne, LRU_WIDTH), 0.02),
        "lru_gate_a_w": nrm(ks[12], (ne, LRU_BLOCKS, LRU_BLOCK, LRU_BLOCK), LRU_BLOCK ** -0.5),
        "lru_gate_a_b": nrm(ks[13], (ne, LRU_WIDTH), 0.02),
        "lru_gate_x_w": nrm(ks[14], (ne, LRU_BLOCKS, LRU_BLOCK, LRU_BLOCK), LRU_BLOCK ** -0.5),
        "lru_gate_x_b": nrm(ks[15], (ne, LRU_WIDTH), 0.02),
        "lru_lambda": jnp.log(root) - jnp.log1p(-root),
        "ffn_w_gate": nrm(ks[17], (ne, D_MODEL, D_FF_DENSE), D_MODEL ** -0.5),
        "ffn_w_up": nrm(ks[18], (ne, D_MODEL, D_FF_DENSE), D_MODEL ** -0.5),
        "ffn_w_down": nrm(ks[19], (ne, D_FF_DENSE, D_MODEL), D_FF_DENSE ** -0.5),
        "odd_w_in": nrm(ks[20], (no, D_MODEL, ODD_IN), D_MODEL ** -0.5),
        "odd_w_out": nrm(ks[21], (no, DIFF_V, D_MODEL), DIFF_V ** -0.5),
        "diff_lambda_q1": nrm(ks[22], (no, HEAD_DIM), 0.1),
        "diff_lambda_k1": nrm(ks[23], (no, HEAD_DIM), 0.1),
        "diff_lambda_q2": nrm(ks[24], (no, HEAD_DIM), 0.1),
        "diff_lambda_k2": nrm(ks[25], (no, HEAD_DIM), 0.1),
        "diff_subln": 1.0 + nrm(ks[26], (no, DIFF_VDIM), 0.1),
        "router_w": nrm(ks[27], (no, D_MODEL, N_EXPERTS), D_MODEL ** -0.5),
        "router_b": nrm(ks[28], (no, N_EXPERTS), 0.01),
        "moe_w_gate": nrm(ks[29], (no, N_EXPERTS, D_MODEL, D_FF_EXPERT), D_MODEL ** -0.5),
        "moe_w_up": nrm(ks[30], (no, N_EXPERTS, D_MODEL, D_FF_EXPERT), D_MODEL ** -0.5),
        "moe_w_down": nrm(ks[31], (no, N_EXPERTS, D_FF_EXPERT, D_MODEL), D_FF_EXPERT ** -0.5),
    }


def reference(x, c, rel_bias, ada_w, ada_b, ln_mix, ln_ffn, ln_final,
              even_w_in, even_w_out, lru_conv_w, lru_conv_b, lru_gate_a_w, lru_gate_a_b,
              lru_gate_x_w, lru_gate_x_b, lru_lambda, ffn_w_gate, ffn_w_up, ffn_w_down,
              odd_w_in, odd_w_out, diff_lambda_q1, diff_lambda_k1, diff_lambda_q2,
              diff_lambda_k2, diff_subln, router_w, router_b, moe_w_gate, moe_w_up, moe_w_down):
    cond = jax.nn.silu(c)
    for layer in range(DEPTH):
        i = layer // 2
        mod = cond @ ada_w[layer] + ada_b[layer]
        sh1, sc1, g1, sh2, sc2, g2 = jnp.split(mod, 6, axis=-1)
        h = modulate(rmsnorm(x, ln_mix[layer]), sh1, sc1)
        if layer % 2 == 0:
            mix = even_mixer(h, even_w_in[i], even_w_out[i], lru_conv_w[i], lru_conv_b[i],
                             lru_gate_a_w[i], lru_gate_a_b[i], lru_gate_x_w[i], lru_gate_x_b[i],
                             lru_lambda[i])
        else:
            lambda_init = 0.8 - 0.6 * math.exp(-0.3 * layer)
            mix = odd_mixer(h, odd_w_in[i], odd_w_out[i], rel_bias, diff_lambda_q1[i],
                            diff_lambda_k1[i], diff_lambda_q2[i], diff_lambda_k2[i],
                            diff_subln[i], lambda_init)
        x = x + g1[:, None, :] * mix
        h = modulate(rmsnorm(x, ln_ffn[layer]), sh2, sc2)
        if layer % 2 == 0:
            ffn = swiglu(h, ffn_w_gate[i], ffn_w_up[i], ffn_w_down[i])
        else:
            ffn = moe_ffn(h, router_w[i], router_b[i], moe_w_gate[i], moe_w_up[i], moe_w_down[i])
        x = x + g2[:, None, :] * ffn
    return rmsnorm(x, ln_final)
```

```python
import functools
import math

import jax
import jax.numpy as jnp
from jax import lax
from jax.experimental import pallas as pl
from jax.experimental.pallas import tpu as pltpu

F32 = jnp.float32
BF16 = jnp.bfloat16

D_MODEL = 1024
HEAD_DIM = 64
LANES = 128
SB_HEADS = 8
SB_WIDTH = SB_HEADS * HEAD_DIM
LRU_WIDTH = 512
LRU_BLOCKS = 8
CONV_WIDTH = 4
LRU_C = 8.0
DIFF_HEADS = 8
DIFF_QK = 2 * DIFF_HEADS * HEAD_DIM
DIFF_V = DIFF_HEADS * 2 * HEAD_DIM
N_BUCKETS = 32
MAX_EXACT = N_BUCKETS // 2
MAX_DISTANCE = 128
D_FF_DENSE = 2816
N_EXPERTS = 8
D_FF_EXPERT = 3584
EPS = 1e-6
NEG_BIG = -1e30

VMEM_LIMIT = 56 * 1024 * 1024


def _cparams(sem):
    return pltpu.CompilerParams(dimension_semantics=sem, vmem_limit_bytes=VMEM_LIMIT)


def _dot(a, b):
    return jnp.dot(a, b, preferred_element_type=F32)


def _dot_nt(a, b):
    return lax.dot_general(a, b, (((1,), (1,)), ((), ())), preferred_element_type=F32)


def _split_bf16(x):
    hi = x.astype(BF16)
    lo = (x - hi.astype(F32)).astype(BF16)
    return hi, lo


def _sigmoid(x):
    return 1.0 / (1.0 + jnp.exp(-x))


def _norm_mod(x, lnw, shift, scale):
    ms = jnp.mean(x * x, axis=-1, keepdims=True)
    n = x * lax.rsqrt(ms + EPS)
    return (n * lnw) * (1.0 + scale) + shift


def _adaln_kernel(c_ref, w_ref, b_ref, o_ref):
    c = c_ref[...]
    cond = c * _sigmoid(c)
    ch, cl = _split_bf16(cond)
    wh, wl = _split_bf16(w_ref[0])
    o_ref[0] = _dot(ch, wh) + _dot(ch, wl) + _dot(cl, wh) + b_ref[0]


def _adaln(c, ada_w, ada_b):
    depth, d, n6 = ada_w.shape
    b = c.shape[0]
    tn = 1536
    return pl.pallas_call(
        _adaln_kernel,
        grid=(depth, n6 // tn),
        in_specs=[
            pl.BlockSpec((b, d), lambda l, j: (0, 0)),
            pl.BlockSpec((1, d, tn), lambda l, j: (l, 0, j)),
            pl.BlockSpec((1, 1, tn), lambda l, j: (l, 0, j)),
        ],
        out_specs=pl.BlockSpec((1, b, tn), lambda l, j: (l, 0, j)),
        out_shape=jax.ShapeDtypeStruct((depth, b, n6), F32),
        compiler_params=_cparams(("arbitrary", "arbitrary")),
        name="adaln",
    )(c, ada_w, ada_b.reshape(depth, 1, n6))


def _inproj_kernel(x_ref, lnw_ref, sh_ref, sc_ref, w_ref, *o_refs, splits):
    h = _norm_mod(x_ref[...], lnw_ref[...], sh_ref[0], sc_ref[0]).astype(BF16)
    for o_ref, (lo, hi) in zip(o_refs, splits):
        o_ref[...] = _dot(h, w_ref[:, lo:hi]).astype(o_ref.dtype)


def _inproj(x, lnw, shift, scale, w, splits, dtypes, seq, tm=512):
    n, d = x.shape
    tiles_per_seq = seq // tm
    bidx = lambda i: (i // tiles_per_seq, 0, 0)
    return pl.pallas_call(
        functools.partial(_inproj_kernel, splits=splits),
        grid=(n // tm,),
        in_specs=[
            pl.BlockSpec((tm, d), lambda i: (i, 0)),
            pl.BlockSpec((1, d), lambda i: (0, 0)),
            pl.BlockSpec((1, 1, d), bidx),
            pl.BlockSpec((1, 1, d), bidx),
            pl.BlockSpec(w.shape, lambda i: (0, 0)),
        ],
        out_specs=[pl.BlockSpec((tm, hi - lo), lambda i: (i, 0)) for lo, hi in splits],
        out_shape=[jax.ShapeDtypeStruct((n, hi - lo), dt) for (lo, hi), dt in zip(splits, dtypes)],
        compiler_params=_cparams(("arbitrary",)),
        name="inproj",
    )(x, lnw, shift, scale, w)


def _sb_kernel(q_ref, k_ref, v_ref, o_ref, acc_ref, c_ref, *, tq, tk):
    qi = pl.program_id(2)
    lane = lax.broadcasted_iota(jnp.int32, (tq, LANES), 1)
    q = q_ref[...] * (HEAD_DIM ** -0.5)
    zero = jnp.zeros_like(q)
    qs = (jnp.where(lane < HEAD_DIM, q, zero), jnp.where(lane >= HEAD_DIM, q, zero))
    rj = lax.broadcasted_iota(jnp.int32, (tk, tk), 0)
    cs = lax.broadcasted_iota(jnp.int32, (tk, tk), 1)
    upper = (rj > cs).astype(BF16)

    acc_ref[...] = jnp.zeros_like(acc_ref)
    c_ref[...] = jnp.zeros_like(c_ref)

    def tile(j, masked):
        start = pl.multiple_of(j * tk, tk)
        ks = k_ref[pl.ds(start, tk), :]
        vs = v_ref[pl.ds(start, tk), :]
        if masked:
            q_pos = qi * tq + lax.broadcasted_iota(jnp.int32, (tq, tk), 0)
            k_pos = j * tk + lax.broadcasted_iota(jnp.int32, (tq, tk), 1)
            mask = k_pos < q_pos
        for hh in range(2):
            z = _dot_nt(qs[hh], ks)
            lk = jnp.minimum(-z, 0.0) - jnp.log(1.0 + jnp.exp(-jnp.abs(z)))
            if masked:
                lk = jnp.where(mask, lk, 0.0)
            hi, lo = _split_bf16(lk)
            later = _dot(hi, upper) + _dot(lo, upper) + c_ref[hh]
            w = jnp.exp((z + lk) + later)
            if masked:
                w = jnp.where(mask, w, 0.0)
            acc_ref[hh] += _dot(w.astype(BF16), vs)
            c_ref[hh] += jnp.sum(lk, axis=1, keepdims=True)

    ratio = tq // tk
    for d in range(ratio):
        tile(qi * ratio + (ratio - 1 - d), True)

    def body(n, carry):
        tile(qi * ratio - 1 - n, False)
        return carry

    lax.fori_loop(0, qi * ratio, body, 0)
    o_ref[...] = jnp.where(lane < HEAD_DIM, acc_ref[0], acc_ref[1]).astype(o_ref.dtype)


def _sb_attention(qkv, batch, seq, tq=512, tk=256):
    tq = min(tq, seq)
    tk = min(tk, tq)
    n = qkv.shape[0]
    pairs = SB_WIDTH // LANES
    nq = seq // tq
    return pl.pallas_call(
        functools.partial(_sb_kernel, tq=tq, tk=tk),
        grid=(batch, pairs, nq),
        in_specs=[
            pl.BlockSpec((tq, LANES), lambda b, p, i: (b * nq + i, p)),
            pl.BlockSpec((seq, LANES), lambda b, p, i: (b, pairs + p)),
            pl.BlockSpec((seq, LANES), lambda b, p, i: (b, 2 * pairs + p)),
        ],
        out_specs=pl.BlockSpec((tq, LANES), lambda b, p, i: (b * nq + i, p)),
        out_shape=jax.ShapeDtypeStruct((n, SB_WIDTH), BF16),
        scratch_shapes=[pltpu.VMEM((2, tq, LANES), F32), pltpu.VMEM((2, tq, 1), F32)],
        compiler_params=_cparams(("arbitrary", "arbitrary", "arbitrary")),
        name="sb_attention",
    )(qkv, qkv, qkv)


def _lru_kernel(xg_ref, cw_ref, cb_ref, wg_ref, bg_ref, lam_ref, o_ref,
                ext_ref, a_ref, u_ref, h_ref, carry_ref, *, tt):
    c = LRU_WIDTH
    halo = 8

    @pl.when(pl.program_id(1) == 0)
    def _():
        ext_ref[0:halo, :] = jnp.zeros((halo, c), F32)
        carry_ref[...] = jnp.zeros_like(carry_ref)

    xb = xg_ref[:, 0:c]
    gb = xg_ref[:, c:2 * c]
    ext_ref[halo:halo + tt, :] = xb
    xc = cb_ref[...] + cw_ref[CONV_WIDTH - 1:CONV_WIDTH, :] * xb
    for i in range(CONV_WIDTH - 1):
        off = halo - (CONV_WIDTH - 1) + i
        xc = xc + cw_ref[i:i + 1, :] * ext_ref[off:off + tt, :]
    ext_ref[0:halo, :] = xb[tt - halo:tt, :]

    gates = _dot(xc.astype(BF16), wg_ref[...]) + bg_ref[...]
    r = _sigmoid(gates[:, 0:c])
    ig = _sigmoid(gates[:, c:2 * c])
    lam = lam_ref[...]
    log_sig_lam = jnp.minimum(lam, 0.0) - jnp.log(1.0 + jnp.exp(-jnp.abs(lam)))
    a = jnp.exp(LRU_C * r * log_sig_lam)
    u = jnp.sqrt(1.0 - a * a) * (ig * xc)

    r8 = lax.broadcasted_iota(jnp.int32, (tt, c), 0) & 7
    for s in (1, 2, 4):
        a_prev = pltpu.roll(a, s, axis=0)
        u_prev = pltpu.roll(u, s, axis=0)
        valid = r8 >= s
        u = jnp.where(valid, a * u_prev + u, u)
        a = jnp.where(valid, a * a_prev, a)
    a_ref[...] = a
    u_ref[...] = u

    def body(g, carry):
        off = pl.multiple_of(g * 8, 8)
        h = a_ref[pl.ds(off, 8), :] * carry + u_ref[pl.ds(off, 8), :]
        h_ref[pl.ds(off, 8), :] = h
        return jnp.broadcast_to(h[7:8, :], (8, c))

    carry_ref[...] = lax.fori_loop(0, tt // 8, body, carry_ref[...])

    gelu = 0.5 * gb * (1.0 + jnp.tanh(math.sqrt(2.0 / math.pi) * (gb + 0.044715 * (gb * gb * gb))))
    o_ref[...] = (h_ref[...] * gelu).astype(o_ref.dtype)


def _block_diag(w):
    g, bi, bo = w.shape
    eye = jnp.eye(g, dtype=w.dtype)
    return (w[:, :, None, :] * eye[:, None, :, None]).reshape(g * bi, g * bo)


def _rg_lru(xg, conv_w, conv_b, ga_w, ga_b, gx_w, gx_b, lam, batch, seq, tt=512):
    tt = min(tt, seq)
    n = xg.shape[0]
    c = LRU_WIDTH
    nt = seq // tt
    wg = jnp.concatenate([_block_diag(ga_w), _block_diag(gx_w)], axis=1).astype(BF16)
    bg = jnp.concatenate([ga_b, gx_b]).reshape(1, 2 * c)
    const = lambda b, t: (0, 0)
    return pl.pallas_call(
        functools.partial(_lru_kernel, tt=tt),
        grid=(batch, nt),
        in_specs=[
            pl.BlockSpec((tt, 2 * c), lambda b, t: (b * nt + t, 0)),
            pl.BlockSpec((CONV_WIDTH, c), const),
            pl.BlockSpec((1, c), const),
            pl.BlockSpec((c, 2 * c), const),
            pl.BlockSpec((1, 2 * c), const),
            pl.BlockSpec((1, c), const),
        ],
        out_specs=pl.BlockSpec((tt, c), lambda b, t: (b * nt + t, 0)),
        out_shape=jax.ShapeDtypeStruct((n, c), BF16),
        scratch_shapes=[
            pltpu.VMEM((tt + 8, c), F32),
            pltpu.VMEM((tt, c), F32),
            pltpu.VMEM((tt, c), F32),
            pltpu.VMEM((tt, c), F32),
            pltpu.VMEM((8, c), F32),
        ],
        compiler_params=_cparams(("arbitrary", "arbitrary")),
        name="rg_lru",
    )(xg, conv_w, conv_b.reshape(1, c), wg, bg, lam.reshape(1, c))


def _outproj_kernel(*refs, widths):
    x_ref, g_ref, w_ref = refs[0], refs[1], refs[2]
    y_refs = refs[3:3 + len(widths)]
    o_ref = refs[3 + len(widths)]
    mix = None
    off = 0
    for y_ref, wd in zip(y_refs, widths):
        part = _dot(y_ref[...], w_ref[off:off + wd, :])
        mix = part if mix is None else mix + part
        off += wd
    o_ref[...] = x_ref[...] + g_ref[0] * mix


def _outproj(x, gate, w, ys, seq, tm=512):
    n, d = x.shape
    tiles_per_seq = seq // tm
    widths = tuple(y.shape[1] for y in ys)
    return pl.pallas_call(
        functools.partial(_outproj_kernel, widths=widths),
        grid=(n // tm,),
        in_specs=[
            pl.BlockSpec((tm, d), lambda i: (i, 0)),
            pl.BlockSpec((1, 1, d), lambda i: (i // tiles_per_seq, 0, 0)),
            pl.BlockSpec(w.shape, lambda i: (0, 0)),
        ] + [pl.BlockSpec((tm, wd), lambda i: (i, 0)) for wd in widths],
        out_specs=pl.BlockSpec((tm, d), lambda i: (i, 0)),
        out_shape=jax.ShapeDtypeStruct((n, d), F32),
        compiler_params=_cparams(("arbitrary",)),
        name="outproj",
    )(x, gate, w, *ys)


def _ffn_kernel(x_ref, lnw_ref, sh_ref, sc_ref, g_ref, wg_ref, wu_ref, wd_ref, o_ref,
                h_ref, acc_ref):
    k = pl.program_id(1)

    @pl.when(k == 0)
    def _():
        h_ref[...] = _norm_mod(x_ref[...], lnw_ref[...], sh_ref[0], sc_ref[0]).astype(BF16)
        acc_ref[...] = jnp.zeros_like(acc_ref)

    h = h_ref[...]
    gt = _dot(h, wg_ref[...])
    up = _dot(h, wu_ref[...])
    act = (gt * _sigmoid(gt) * up).astype(BF16)
    acc_ref[...] += _dot(act, wd_ref[...])

    @pl.when(k == pl.num_programs(1) - 1)
    def _():
        o_ref[...] = x_ref[...] + g_ref[0] * acc_ref[...]


def _ffn(x, lnw, shift, scale, gate, wg, wu, wd, seq, tm=512, tf=1408):
    n, d = x.shape
    f = wg.shape[1]
    tiles_per_seq = seq // tm
    bidx = lambda i, k: (i // tiles_per_seq, 0, 0)
    return pl.pallas_call(
        _ffn_kernel,
        grid=(n // tm, f // tf),
        in_specs=[
            pl.BlockSpec((tm, d), lambda i, k: (i, 0)),
            pl.BlockSpec((1, d), lambda i, k: (0, 0)),
            pl.BlockSpec((1, 1, d), bidx),
            pl.BlockSpec((1, 1, d), bidx),
            pl.BlockSpec((1, 1, d), bidx),
            pl.BlockSpec((d, tf), lambda i, k: (0, k)),
            pl.BlockSpec((d, tf), lambda i, k: (0, k)),
            pl.BlockSpec((tf, d), lambda i, k: (k, 0)),
        ],
        out_specs=pl.BlockSpec((tm, d), lambda i, k: (i, 0)),
        out_shape=jax.ShapeDtypeStruct((n, d), F32),
        scratch_shapes=[pltpu.VMEM((tm, d), BF16), pltpu.VMEM((tm, d), F32)],
        compiler_params=_cparams(("arbitrary", "arbitrary")),
        name="ffn_dense",
    )(x, lnw, shift, scale, gate, wg, wu, wd)


def _bias_kernel(rb_ref, o_ref, *, tq, tk, n_near):
    h = pl.program_id(0)
    t = pl.program_id(1)
    row = lax.broadcasted_iota(jnp.int32, (tq, tk), 0)
    col = lax.broadcasted_iota(jnp.int32, (tq, tk), 1)
    rel = row - col + (n_near - 1 - t) * tk - (tq - tk)
    nn = jnp.maximum(rel, 0)
    nf = jnp.maximum(nn, 1).astype(F32)
    large = MAX_EXACT + (jnp.log(nf / MAX_EXACT) / math.log(MAX_DISTANCE / MAX_EXACT)
                         * (N_BUCKETS - MAX_EXACT)).astype(jnp.int32)
    large = jnp.minimum(large, N_BUCKETS - 1)
    bucket = jnp.where(nn < MAX_EXACT, nn, large)
    bias = jnp.zeros((tq, tk), F32)
    for b in range(N_BUCKETS):
        bias = jnp.where(bucket == b, rb_ref[b, h], bias)
    o_ref[0, 0] = jnp.where(rel >= 0, bias, NEG_BIG)


def _near_bias(rel_bias, tq, tk, n_near):
    return pl.pallas_call(
        functools.partial(_bias_kernel, tq=tq, tk=tk, n_near=n_near),
        grid=(DIFF_HEADS, n_near),
        in_specs=[pl.BlockSpec(memory_space=pltpu.SMEM)],
        out_specs=pl.BlockSpec((1, 1, tq, tk), lambda h, t: (h, t, 0, 0)),
        out_shape=jax.ShapeDtypeStruct((DIFF_HEADS, n_near, tq, tk), F32),
        compiler_params=_cparams(("arbitrary", "arbitrary")),
        name="t5_bias_tiles",
    )(rel_bias)


def _diff_kernel(rb_ref, q_ref, k_ref, v_ref, nb_ref, lq1_ref, lk1_ref, lq2_ref, lk2_ref,
                 sub_ref, o_ref, m_ref, l_ref, acc_ref, *, tq, tk, n_near, lambda_init):
    h = pl.program_id(1)
    qi = pl.program_id(2)
    ratio = tq // tk
    lane = lax.broadcasted_iota(jnp.int32, (tq, LANES), 1)
    q = q_ref[...] * (HEAD_DIM ** -0.5)
    zero = jnp.zeros_like(q)
    qs = (jnp.where(lane < HEAD_DIM, q, zero), jnp.where(lane >= HEAD_DIM, q, zero))

    m_ref[...] = jnp.full_like(m_ref, NEG_BIG)
    l_ref[...] = jnp.zeros_like(l_ref)
    acc_ref[...] = jnp.zeros_like(acc_ref)

    def tile(j, bias):
        start = pl.multiple_of(j * tk, tk)
        ks = k_ref[pl.ds(start, tk), :]
        vs = v_ref[pl.ds(start, tk), :]
        for mm in range(2):
            z = _dot_nt(qs[mm], ks) + bias
            m_old = m_ref[mm]
            m_new = jnp.maximum(m_old, jnp.max(z, axis=1, keepdims=True))
            alpha = jnp.exp(m_old - m_new)
            p = jnp.exp(z - m_new)
            l_ref[mm] = alpha * l_ref[mm] + jnp.sum(p, axis=1, keepdims=True)
            acc_ref[mm] = alpha * acc_ref[mm] + _dot(p.astype(BF16), vs)
            m_ref[mm] = m_new

    n_far = jnp.maximum((qi + 1) * ratio - n_near, 0)
    far_bias = rb_ref[N_BUCKETS - 1, h]

    def body(j, carry):
        tile(j, far_bias)
        return carry

    lax.fori_loop(0, n_far, body, 0)

    for t in range(n_near):
        j = (qi + 1) * ratio - n_near + t

        @pl.when(j >= 0)
        def _():
            tile(j, nb_ref[0, t])

    lam = (jnp.exp(jnp.sum(lq1_ref[...] * lk1_ref[...], axis=1, keepdims=True))
           - jnp.exp(jnp.sum(lq2_ref[...] * lk2_ref[...], axis=1, keepdims=True))
           + lambda_init)
    o = acc_ref[0] / l_ref[0] - lam * (acc_ref[1] / l_ref[1])
    ms = jnp.mean(o * o, axis=-1, keepdims=True)
    o = o * lax.rsqrt(ms + EPS) * sub_ref[...] * (1.0 - lambda_init)
    o_ref[...] = o.astype(o_ref.dtype)


def _diff_attention(qkv, rel_bias, lq1, lk1, lq2, lk2, subln, lambda_init, batch, seq,
                    tq=256, tk=256):
    tq = min(tq, seq)
    tk = min(tk, tq)
    assert tk >= MAX_DISTANCE
    n = qkv.shape[0]
    nq = seq // tq
    nh = DIFF_HEADS
    n_near = tq // tk + 1
    near = _near_bias(rel_bias, tq, tk, n_near)
    vec = lambda a: a.reshape(1, -1)
    cvec = pl.BlockSpec((1, HEAD_DIM), lambda b, h, i: (0, 0))
    return pl.pallas_call(
        functools.partial(_diff_kernel, tq=tq, tk=tk, n_near=n_near, lambda_init=lambda_init),
        grid=(batch, nh, nq),
        in_specs=[
            pl.BlockSpec(memory_space=pltpu.SMEM),
            pl.BlockSpec((tq, LANES), lambda b, h, i: (b * nq + i, h)),
            pl.BlockSpec((seq, LANES), lambda b, h, i: (b, nh + h)),
            pl.BlockSpec((seq, LANES), lambda b, h, i: (b, 2 * nh + h)),
            pl.BlockSpec((1, n_near, tq, tk), lambda b, h, i: (h, 0, 0, 0)),
            cvec, cvec, cvec, cvec,
            pl.BlockSpec((1, LANES), lambda b, h, i: (0, 0)),
        ],
        out_specs=pl.BlockSpec((tq, LANES), lambda b, h, i: (b * nq + i, h)),
        out_shape=jax.ShapeDtypeStruct((n, DIFF_V), BF16),
        scratch_shapes=[
            pltpu.VMEM((2, tq, 1), F32),
            pltpu.VMEM((2, tq, 1), F32),
            pltpu.VMEM((2, tq, LANES), F32),
        ],
        compiler_params=_cparams(("arbitrary", "arbitrary", "arbitrary")),
        name="diff_attention",
    )(rel_bias, qkv, qkv, qkv, near, vec(lq1), vec(lk1), vec(lq2), vec(lk2), vec(subln))


def _moe_kernel(x_ref, lnw_ref, sh_ref, sc_ref, g_ref, rw_ref, rb_ref, lnf_ref,
                wg_ref, wu_ref, wd_ref, o_ref, h_ref, gates_ref, acc_ref, y_ref, *, tm):
    e = pl.program_id(1)
    k = pl.program_id(2)
    last_k = pl.num_programs(2) - 1
    lane = lax.broadcasted_iota(jnp.int32, (tm, LANES), 1)

    @pl.when((e == 0) & (k == 0))
    def _():
        h = _norm_mod(x_ref[...], lnw_ref[...], sh_ref[0], sc_ref[0])
        h_ref[...] = h.astype(BF16)
        hh, hl = _split_bf16(h)
        wh, wl = _split_bf16(rw_ref[...])
        logits = _dot(hh, wh) + _dot(hl, wh) + _dot(hh, wl) + rb_ref[...]
        neg_inf = jnp.float32(-jnp.inf)
        lanef = lane.astype(F32)
        logits = jnp.where(lane < N_EXPERTS, logits, neg_inf)
        m1 = jnp.max(logits, axis=1, keepdims=True)
        i1 = jnp.min(jnp.where(logits == m1, lanef, float(LANES)), axis=1, keepdims=True)
        rest = jnp.where(lanef == i1, neg_inf, logits)
        m2 = jnp.max(rest, axis=1, keepdims=True)
        i2 = jnp.min(jnp.where(rest == m2, lanef, float(LANES)), axis=1, keepdims=True)
        w1 = 1.0 / (1.0 + jnp.exp(m2 - m1))
        w2 = 1.0 - w1
        gates_ref[...] = jnp.where(lanef == i1, w1, 0.0) + jnp.where(lanef == i2, w2, 0.0)
        y_ref[...] = jnp.zeros_like(y_ref)

    @pl.when(k == 0)
    def _():
        acc_ref[...] = jnp.zeros_like(acc_ref)

    h = h_ref[...]
    gt = _dot(h, wg_ref[0])
    up = _dot(h, wu_ref[0])
    act = (gt * _sigmoid(gt) * up).astype(BF16)
    acc_ref[...] += _dot(act, wd_ref[0])

    @pl.when(k == last_k)
    def _():
        ge = jnp.sum(jnp.where(lane == e, gates_ref[...], 0.0), axis=1, keepdims=True)
        y_ref[...] += ge * acc_ref[...]

    @pl.when((e == pl.num_programs(1) - 1) & (k == last_k))
    def _():
        xo = x_ref[...] + g_ref[0] * y_ref[...]
        ms = jnp.mean(xo * xo, axis=-1, keepdims=True)
        o_ref[...] = xo * lax.rsqrt(ms + EPS) * lnf_ref[...]


def _moe(x, lnw, shift, scale, gate, router_w, router_b, ln_final, wg, wu, wd, seq,
         tm=512, tf=896):
    n, d = x.shape
    ne, _, f = wg.shape
    tiles_per_seq = seq // tm
    rw = jnp.zeros((d, LANES), F32).at[:, :ne].set(router_w)
    rb = jnp.zeros((1, LANES), F32).at[0, :ne].set(router_b)
    bidx = lambda i, e, k: (i // tiles_per_seq, 0, 0)
    const = lambda i, e, k: (0, 0)
    return pl.pallas_call(
        functools.partial(_moe_kernel, tm=tm),
        grid=(n // tm, ne, f // tf),
        in_specs=[
            pl.BlockSpec((tm, d), lambda i, e, k: (i, 0)),
            pl.BlockSpec((1, d), const),
            pl.BlockSpec((1, 1, d), bidx),
            pl.BlockSpec((1, 1, d), bidx),
            pl.BlockSpec((1, 1, d), bidx),
            pl.BlockSpec((d, LANES), const),
            pl.BlockSpec((1, LANES), const),
            pl.BlockSpec((1, d), const),
            pl.BlockSpec((1, d, tf), lambda i, e, k: (e, 0, k)),
            pl.BlockSpec((1, d, tf), lambda i, e, k: (e, 0, k)),
            pl.BlockSpec((1, tf, d), lambda i, e, k: (e, k, 0)),
        ],
        out_specs=pl.BlockSpec((tm, d), lambda i, e, k: (i, 0)),
        out_shape=jax.ShapeDtypeStruct((n, d), F32),
        scratch_shapes=[
            pltpu.VMEM((tm, d), BF16),
            pltpu.VMEM((tm, LANES), F32),
            pltpu.VMEM((tm, d), F32),
            pltpu.VMEM((tm, d), F32),
        ],
        compiler_params=_cparams(("arbitrary", "arbitrary", "arbitrary")),
        name="moe_ffn",
    )(x, lnw, shift, scale, gate, rw, rb, ln_final, wg, wu, wd)


def kernel(x, c, rel_bias, ada_w, ada_b, ln_mix, ln_ffn, ln_final, even_w_in, even_w_out, lru_conv_w, lru_conv_b, lru_gate_a_w, lru_gate_a_b, lru_gate_x_w, lru_gate_x_b, lru_lambda, ffn_w_gate, ffn_w_up, ffn_w_down, odd_w_in, odd_w_out, diff_lambda_q1, diff_lambda_k1, diff_lambda_q2, diff_lambda_k2, diff_subln, router_w, router_b, moe_w_gate, moe_w_up, moe_w_down):
    batch, seq, d = x.shape
    n = batch * seq
    xf = x.reshape(n, d)
    mod = _adaln(c, ada_w, ada_b)
    part = lambda layer, j: mod[layer, :, j * d:(j + 1) * d].reshape(batch, 1, d)
    row = lambda v: v.reshape(1, -1)

    sh1, sc1, g1, sh2, sc2, g2 = (part(0, j) for j in range(6))
    qkv_w = 3 * SB_WIDTH
    qkv, xg = _inproj(xf, row(ln_mix[0]), sh1, sc1, even_w_in[0].astype(BF16),
                      ((0, qkv_w), (qkv_w, qkv_w + 2 * LRU_WIDTH)), (BF16, F32), seq)
    ya = _sb_attention(qkv, batch, seq)
    yb = _rg_lru(xg, lru_conv_w[0], lru_conv_b[0], lru_gate_a_w[0], lru_gate_a_b[0],
                 lru_gate_x_w[0], lru_gate_x_b[0], lru_lambda[0], batch, seq)
    xf = _outproj(xf, g1, even_w_out[0].astype(BF16), (ya, yb), seq)
    xf = _ffn(xf, row(ln_ffn[0]), sh2, sc2, g2, ffn_w_gate[0].astype(BF16),
              ffn_w_up[0].astype(BF16), ffn_w_down[0].astype(BF16), seq)

    sh1, sc1, g1, sh2, sc2, g2 = (part(1, j) for j in range(6))
    lambda_init = 0.8 - 0.6 * math.exp(-0.3 * 1)
    (qkv,) = _inproj(xf, row(ln_mix[1]), sh1, sc1, odd_w_in[0].astype(BF16),
                     ((0, 2 * DIFF_QK + DIFF_V),), (BF16,), seq)
    yo = _diff_attention(qkv, rel_bias, diff_lambda_q1[0], diff_lambda_k1[0], diff_lambda_q2[0],
                         diff_lambda_k2[0], diff_subln[0], lambda_init, batch, seq)
    xf = _outproj(xf, g1, odd_w_out[0].astype(BF16), (yo,), seq)
    out = _moe(xf, row(ln_ffn[1]), sh2, sc2, g2, router_w[0], router_b[0], row(ln_final),
               moe_w_gate[0].astype(BF16), moe_w_up[0].astype(BF16), moe_w_down[0].astype(BF16), seq)
    return out.reshape(batch, seq, d)
```

```python
import functools
import math

import jax
import jax.numpy as jnp
from jax import lax
from jax.experimental import pallas as pl
from jax.experimental.pallas import tpu as pltpu

F32 = jnp.float32
BF16 = jnp.bfloat16

D_MODEL = 1024
HEAD_DIM = 64
LANES = 128
SB_HEADS = 8
SB_WIDTH = SB_HEADS * HEAD_DIM
LRU_WIDTH = 512
LRU_BLOCKS = 8
CONV_WIDTH = 4
LRU_C = 8.0
DIFF_HEADS = 8
DIFF_QK = 2 * DIFF_HEADS * HEAD_DIM
DIFF_V = DIFF_HEADS * 2 * HEAD_DIM
N_BUCKETS = 32
MAX_EXACT = N_BUCKETS // 2
MAX_DISTANCE = 128
D_FF_DENSE = 2816
N_EXPERTS = 8
D_FF_EXPERT = 3584
EPS = 1e-6
NEG_BIG = -1e30
UNDERFLOW_LOG = 110.0

VMEM_LIMIT = 56 * 1024 * 1024


def _cparams(sem):
    return pltpu.CompilerParams(dimension_semantics=sem, vmem_limit_bytes=VMEM_LIMIT)


def _dot(a, b):
    return jnp.dot(a, b, preferred_element_type=F32)


def _dot_nt(a, b):
    return lax.dot_general(a, b, (((1,), (1,)), ((), ())), preferred_element_type=F32)


def _split_bf16(x):
    hi = x.astype(BF16)
    lo = (x - hi.astype(F32)).astype(BF16)
    return hi, lo


def _sigmoid(x):
    return 1.0 / (1.0 + jnp.exp(-x))


def _norm_mod(x, lnw, shift, scale):
    ms = jnp.mean(x * x, axis=-1, keepdims=True)
    n = x * lax.rsqrt(ms + EPS)
    return (n * lnw) * (1.0 + scale) + shift


def _adaln_kernel(c_ref, w_ref, b_ref, o_ref):
    c = c_ref[...]
    cond = c * _sigmoid(c)
    ch, cl = _split_bf16(cond)
    wh, wl = _split_bf16(w_ref[0])
    o_ref[0] = _dot(ch, wh) + _dot(ch, wl) + _dot(cl, wh) + b_ref[0]


def _adaln(c, ada_w, ada_b):
    depth, d, n6 = ada_w.shape
    b = c.shape[0]
    tn = 1536
    return pl.pallas_call(
        _adaln_kernel,
        grid=(depth, n6 // tn),
        in_specs=[
            pl.BlockSpec((b, d), lambda l, j: (0, 0)),
            pl.BlockSpec((1, d, tn), lambda l, j: (l, 0, j)),
            pl.BlockSpec((1, 1, tn), lambda l, j: (l, 0, j)),
        ],
        out_specs=pl.BlockSpec((1, b, tn), lambda l, j: (l, 0, j)),
        out_shape=jax.ShapeDtypeStruct((depth, b, n6), F32),
        compiler_params=_cparams(("arbitrary", "arbitrary")),
        name="adaln",
    )(c, ada_w, ada_b.reshape(depth, 1, n6))


def _inproj_kernel(x_ref, lnw_ref, sh_ref, sc_ref, w_ref, *o_refs, splits):
    h = _norm_mod(x_ref[...], lnw_ref[...], sh_ref[0], sc_ref[0]).astype(BF16)
    for o_ref, (lo, hi) in zip(o_refs, splits):
        o_ref[...] = _dot(h, w_ref[:, lo:hi]).astype(o_ref.dtype)


def _inproj(x, lnw, shift, scale, w, splits, dtypes, seq, tm=512):
    n, d = x.shape
    tiles_per_seq = seq // tm
    bidx = lambda i: (i // tiles_per_seq, 0, 0)
    return pl.pallas_call(
        functools.partial(_inproj_kernel, splits=splits),
        grid=(n // tm,),
        in_specs=[
            pl.BlockSpec((tm, d), lambda i: (i, 0)),
            pl.BlockSpec((1, d), lambda i: (0, 0)),
            pl.BlockSpec((1, 1, d), bidx),
            pl.BlockSpec((1, 1, d), bidx),
            pl.BlockSpec(w.shape, lambda i: (0, 0)),
        ],
        out_specs=[pl.BlockSpec((tm, hi - lo), lambda i: (i, 0)) for lo, hi in splits],
        out_shape=[jax.ShapeDtypeStruct((n, hi - lo), dt) for (lo, hi), dt in zip(splits, dtypes)],
        compiler_params=_cparams(("arbitrary",)),
        name="inproj",
    )(x, lnw, shift, scale, w)


def _sb_kernel(q_ref, k_ref, v_ref, o_ref, acc_ref, c_ref, *, tq, tk):
    qi = pl.program_id(2)
    ratio = tq // tk
    n_tiles = (qi + 1) * ratio
    nc = tk // LANES
    lane = lax.broadcasted_iota(jnp.int32, (tq, LANES), 1)
    q = q_ref[...] * (HEAD_DIM ** -0.5)
    zero = jnp.zeros_like(q)
    qs = (jnp.where(lane < HEAD_DIM, q, zero), jnp.where(lane >= HEAD_DIM, q, zero))
    rj = lax.broadcasted_iota(jnp.int32, (tk, tk), 0)
    cs = lax.broadcasted_iota(jnp.int32, (tk, tk), 1)
    neg_upper = jnp.where(rj > cs, -1.0, 0.0).astype(BF16)

    acc_ref[...] = jnp.zeros_like(acc_ref)
    c_ref[...] = jnp.zeros_like(c_ref)

    def causal_mask(j):
        q_pos = qi * tq + lax.broadcasted_iota(jnp.int32, (tq, tk), 0)
        k_pos = j * tk + lax.broadcasted_iota(jnp.int32, (tq, tk), 1)
        return k_pos < q_pos

    def tile(j, masked):
        start = pl.multiple_of(j * tk, tk)
        ks = k_ref[pl.ds(start, tk), :]
        vs = v_ref[pl.ds(start, tk), :]
        for hh in range(2):
            z = _dot_nt(qs[hh], ks)
            sp = jnp.maximum(z, 0.0) + jnp.log(1.0 + jnp.exp(-jnp.abs(z)))
            if masked:
                sp = jnp.where(causal_mask(j), sp, 0.0)
            hi, lo = _split_bf16(sp)
            later = _dot(hi, neg_upper) + _dot(lo, neg_upper)
            c = c_ref[hh]
            w = jnp.concatenate(
                [jnp.exp((z - sp)[:, i * LANES:(i + 1) * LANES]
                         + later[:, i * LANES:(i + 1) * LANES] - c) for i in range(nc)], axis=1)
            if masked:
                w = jnp.where(causal_mask(j), w, 0.0)
            acc_ref[hh] += _dot(w.astype(BF16), vs)
            c_ref[hh] = c + jnp.broadcast_to(jnp.sum(sp, axis=1, keepdims=True), (tq, LANES))

    for d in range(ratio):
        tile(n_tiles - 1 - d, True)

    def cond(state):
        n, c_min = state
        return (n < n_tiles - ratio) & (c_min < UNDERFLOW_LOG)

    def body(state):
        n, _ = state
        tile(n_tiles - ratio - 1 - n, False)
        return n + 1, jnp.min(c_ref[...])

    lax.while_loop(cond, body, (jnp.int32(0), jnp.min(c_ref[...])))

    o_ref[...] = jnp.where(lane < HEAD_DIM, acc_ref[0], acc_ref[1]).astype(o_ref.dtype)


def _sb_attention(qkv, batch, seq, tq=512, tk=256):
    tq = min(tq, seq)
    tk = min(tk, tq)
    n = qkv.shape[0]
    pairs = SB_WIDTH // LANES
    nq = seq // tq
    return pl.pallas_call(
        functools.partial(_sb_kernel, tq=tq, tk=tk),
        grid=(batch, pairs, nq),
        in_specs=[
            pl.BlockSpec((tq, LANES), lambda b, p, i: (b * nq + i, p)),
            pl.BlockSpec((seq, LANES), lambda b, p, i: (b, pairs + p)),
            pl.BlockSpec((seq, LANES), lambda b, p, i: (b, 2 * pairs + p)),
        ],
        out_specs=pl.BlockSpec((tq, LANES), lambda b, p, i: (b * nq + i, p)),
        out_shape=jax.ShapeDtypeStruct((n, SB_WIDTH), BF16),
        scratch_shapes=[
            pltpu.VMEM((2, tq, LANES), F32),
            pltpu.VMEM((2, tq, LANES), F32),
        ],
        compiler_params=_cparams(("arbitrary", "arbitrary", "arbitrary")),
        name="sb_attention",
    )(qkv, qkv, qkv)


def _lru_kernel(xg_ref, cw_ref, cb_ref, wg_ref, bg_ref, lam_ref, o_ref,
                ext_ref, a_ref, u_ref, h_ref, carry_ref, *, tt):
    c = LRU_WIDTH
    halo = 8

    @pl.when(pl.program_id(1) == 0)
    def _():
        ext_ref[0:halo, :] = jnp.zeros((halo, c), F32)
        carry_ref[...] = jnp.zeros_like(carry_ref)

    xb = xg_ref[:, 0:c]
    gb = xg_ref[:, c:2 * c]
    ext_ref[halo:halo + tt, :] = xb
    xc = cb_ref[...] + cw_ref[CONV_WIDTH - 1:CONV_WIDTH, :] * xb
    for i in range(CONV_WIDTH - 1):
        off = halo - (CONV_WIDTH - 1) + i
        xc = xc + cw_ref[i:i + 1, :] * ext_ref[off:off + tt, :]
    ext_ref[0:halo, :] = xb[tt - halo:tt, :]

    gates = _dot(xc.astype(BF16), wg_ref[...]) + bg_ref[...]
    r = _sigmoid(gates[:, 0:c])
    ig = _sigmoid(gates[:, c:2 * c])
    lam = lam_ref[...]
    log_sig_lam = jnp.minimum(lam, 0.0) - jnp.log(1.0 + jnp.exp(-jnp.abs(lam)))
    a = jnp.exp(LRU_C * r * log_sig_lam)
    u = jnp.sqrt(1.0 - a * a) * (ig * xc)

    r8 = lax.broadcasted_iota(jnp.int32, (tt, c), 0) & 7
    for s in (1, 2, 4):
        a_prev = pltpu.roll(a, s, axis=0)
        u_prev = pltpu.roll(u, s, axis=0)
        valid = r8 >= s
        u = jnp.where(valid, a * u_prev + u, u)
        a = jnp.where(valid, a * a_prev, a)
    a_ref[...] = a
    u_ref[...] = u

    def body(g, carry):
        off = pl.multiple_of(g * 8, 8)
        h = a_ref[pl.ds(off, 8), :] * carry + u_ref[pl.ds(off, 8), :]
        h_ref[pl.ds(off, 8), :] = h
        return jnp.broadcast_to(h[7:8, :], (8, c))

    carry_ref[...] = lax.fori_loop(0, tt // 8, body, carry_ref[...])

    gelu = 0.5 * gb * (1.0 + jnp.tanh(math.sqrt(2.0 / math.pi) * (gb + 0.044715 * (gb * gb * gb))))
    o_ref[...] = (h_ref[...] * gelu).astype(o_ref.dtype)


def _block_diag(w):
    g, bi, bo = w.shape
    eye = jnp.eye(g, dtype=w.dtype)
    return (w[:, :, None, :] * eye[:, None, :, None]).reshape(g * bi, g * bo)


def _rg_lru(xg, conv_w, conv_b, ga_w, ga_b, gx_w, gx_b, lam, batch, seq, tt=512):
    tt = min(tt, seq)
    n = xg.shape[0]
    c = LRU_WIDTH
    nt = seq // tt
    wg = jnp.concatenate([_block_diag(ga_w), _block_diag(gx_w)], axis=1).astype(BF16)
    bg = jnp.concatenate([ga_b, gx_b]).reshape(1, 2 * c)
    const = lambda b, t: (0, 0)
    return pl.pallas_call(
        functools.partial(_lru_kernel, tt=tt),
        grid=(batch, nt),
        in_specs=[
            pl.BlockSpec((tt, 2 * c), lambda b, t: (b * nt + t, 0)),
            pl.BlockSpec((CONV_WIDTH, c), const),
            pl.BlockSpec((1, c), const),
            pl.BlockSpec((c, 2 * c), const),
            pl.BlockSpec((1, 2 * c), const),
            pl.BlockSpec((1, c), const),
        ],
        out_specs=pl.BlockSpec((tt, c), lambda b, t: (b * nt + t, 0)),
        out_shape=jax.ShapeDtypeStruct((n, c), BF16),
        scratch_shapes=[
            pltpu.VMEM((tt + 8, c), F32),
            pltpu.VMEM((tt, c), F32),
            pltpu.VMEM((tt, c), F32),
            pltpu.VMEM((tt, c), F32),
            pltpu.VMEM((8, c), F32),
        ],
        compiler_params=_cparams(("arbitrary", "arbitrary")),
        name="rg_lru",
    )(xg, conv_w, conv_b.reshape(1, c), wg, bg, lam.reshape(1, c))


def _outproj_kernel(*refs, widths):
    x_ref, g_ref, w_ref = refs[0], refs[1], refs[2]
    y_refs = refs[3:3 + len(widths)]
    o_ref = refs[3 + len(widths)]
    mix = None
    off = 0
    for y_ref, wd in zip(y_refs, widths):
        part = _dot(y_ref[...], w_ref[off:off + wd, :])
        mix = part if mix is None else mix + part
        off += wd
    o_ref[...] = x_ref[...] + g_ref[0] * mix


def _outproj(x, gate, w, ys, seq, tm=512):
    n, d = x.shape
    tiles_per_seq = seq // tm
    widths = tuple(y.shape[1] for y in ys)
    return pl.pallas_call(
        functools.partial(_outproj_kernel, widths=widths),
        grid=(n // tm,),
        in_specs=[
            pl.BlockSpec((tm, d), lambda i: (i, 0)),
            pl.BlockSpec((1, 1, d), lambda i: (i // tiles_per_seq, 0, 0)),
            pl.BlockSpec(w.shape, lambda i: (0, 0)),
        ] + [pl.BlockSpec((tm, wd), lambda i: (i, 0)) for wd in widths],
        out_specs=pl.BlockSpec((tm, d), lambda i: (i, 0)),
        out_shape=jax.ShapeDtypeStruct((n, d), F32),
        compiler_params=_cparams(("arbitrary",)),
        name="outproj",
    )(x, gate, w, *ys)


def _ffn_kernel(x_ref, lnw_ref, sh_ref, sc_ref, g_ref, wg_ref, wu_ref, wd_ref, o_ref,
                h_ref, acc_ref):
    k = pl.program_id(1)

    @pl.when(k == 0)
    def _():
        h_ref[...] = _norm_mod(x_ref[...], lnw_ref[...], sh_ref[0], sc_ref[0]).astype(BF16)
        acc_ref[...] = jnp.zeros_like(acc_ref)

    h = h_ref[...]
    gt = _dot(h, wg_ref[...])
    up = _dot(h, wu_ref[...])
    act = (gt * _sigmoid(gt) * up).astype(BF16)
    acc_ref[...] += _dot(act, wd_ref[...])

    @pl.when(k == pl.num_programs(1) - 1)
    def _():
        o_ref[...] = x_ref[...] + g_ref[0] * acc_ref[...]


def _ffn(x, lnw, shift, scale, gate, wg, wu, wd, seq, tm=512, tf=1408):
    n, d = x.shape
    f = wg.shape[1]
    tiles_per_seq = seq // tm
    bidx = lambda i, k: (i // tiles_per_seq, 0, 0)
    return pl.pallas_call(
        _ffn_kernel,
        grid=(n // tm, f // tf),
        in_specs=[
            pl.BlockSpec((tm, d), lambda i, k: (i, 0)),
            pl.BlockSpec((1, d), lambda i, k: (0, 0)),
            pl.BlockSpec((1, 1, d), bidx),
            pl.BlockSpec((1, 1, d), bidx),
            pl.BlockSpec((1, 1, d), bidx),
            pl.BlockSpec((d, tf), lambda i, k: (0, k)),
            pl.BlockSpec((d, tf), lambda i, k: (0, k)),
            pl.BlockSpec((tf, d), lambda i, k: (k, 0)),
        ],
        out_specs=pl.BlockSpec((tm, d), lambda i, k: (i, 0)),
        out_shape=jax.ShapeDtypeStruct((n, d), F32),
        scratch_shapes=[pltpu.VMEM((tm, d), BF16), pltpu.VMEM((tm, d), F32)],
        compiler_params=_cparams(("arbitrary", "arbitrary")),
        name="ffn_dense",
    )(x, lnw, shift, scale, gate, wg, wu, wd)


def _bias_kernel(rb_ref, o_ref, *, tq, tk, n_near):
    h = pl.program_id(0)
    t = pl.program_id(1)
    row = lax.broadcasted_iota(jnp.int32, (tq, tk), 0)
    col = lax.broadcasted_iota(jnp.int32, (tq, tk), 1)
    rel = row - col + (n_near - 1 - t) * tk - (tq - tk)
    nn = jnp.maximum(rel, 0)
    nf = jnp.maximum(nn, 1).astype(F32)
    large = MAX_EXACT + (jnp.log(nf / MAX_EXACT) / math.log(MAX_DISTANCE / MAX_EXACT)
                         * (N_BUCKETS - MAX_EXACT)).astype(jnp.int32)
    large = jnp.minimum(large, N_BUCKETS - 1)
    bucket = jnp.where(nn < MAX_EXACT, nn, large)
    bias = jnp.zeros((tq, tk), F32)
    for b in range(N_BUCKETS):
        bias = jnp.where(bucket == b, rb_ref[b, h], bias)
    o_ref[0, 0] = jnp.where(rel >= 0, bias, NEG_BIG)


def _near_bias(rel_bias, tq, tk, n_near):
    return pl.pallas_call(
        functools.partial(_bias_kernel, tq=tq, tk=tk, n_near=n_near),
        grid=(DIFF_HEADS, n_near),
        in_specs=[pl.BlockSpec(memory_space=pltpu.SMEM)],
        out_specs=pl.BlockSpec((1, 1, tq, tk), lambda h, t: (h, t, 0, 0)),
        out_shape=jax.ShapeDtypeStruct((DIFF_HEADS, n_near, tq, tk), F32),
        compiler_params=_cparams(("arbitrary", "arbitrary")),
        name="t5_bias_tiles",
    )(rel_bias)


def _diff_kernel(rb_ref, q_ref, k_ref, v_ref, nb_ref, lq1_ref, lk1_ref, lq2_ref, lk2_ref,
                 sub_ref, o_ref, m_ref, acc_ref, vext_ref, z_ref, *, tq, tk, n_near,
                 lambda_init):
    h = pl.program_id(1)
    qi = pl.program_id(2)
    ratio = tq // tk
    lane = lax.broadcasted_iota(jnp.int32, (tq, LANES), 1)
    q = q_ref[...] * (HEAD_DIM ** -0.5)
    zero = jnp.zeros_like(q)
    qs = (jnp.where(lane < HEAD_DIM, q, zero), jnp.where(lane >= HEAD_DIM, q, zero))

    @pl.when(qi == 0)
    def _():
        vext_ref[:, 0:LANES] = v_ref[...]
        vext_ref[:, LANES:2 * LANES] = jnp.ones(v_ref.shape, BF16)

    m_ref[...] = jnp.full_like(m_ref, NEG_BIG)
    acc_ref[...] = jnp.zeros_like(acc_ref)
    nc = tk // LANES

    def scores(j, slot):
        start = pl.multiple_of(j * tk, tk)
        ks = k_ref[pl.ds(start, tk), :]
        for mm in range(2):
            z_ref[slot, mm] = _dot_nt(qs[mm], ks)

    def softmax_pv(j, slot, bias, bias_is_scalar):
        start = pl.multiple_of(j * tk, tk)
        vs = vext_ref[pl.ds(start, tk), :]
        for mm in range(2):
            zc = [z_ref[slot, mm, :, c * LANES:(c + 1) * LANES] for c in range(nc)]
            if not bias_is_scalar:
                zc = [zc[c] + bias[:, c * LANES:(c + 1) * LANES] for c in range(nc)]
            part = functools.reduce(jnp.maximum, zc)
            rm = jnp.broadcast_to(jnp.max(part, axis=1, keepdims=True), (tq, LANES))
            m_old = m_ref[mm]
            if bias_is_scalar:
                m_new = jnp.maximum(m_old, rm + bias)
                shift = m_new - bias
            else:
                m_new = jnp.maximum(m_old, rm)
                shift = m_new
            alpha = jnp.exp(m_old - m_new)
            p = jnp.concatenate([jnp.exp(z - shift) for z in zc], axis=1).astype(BF16)
            pv = _dot(p, vs)
            for c in range(2):
                sl = slice(c * LANES, (c + 1) * LANES)
                acc_ref[mm, :, sl] = alpha * acc_ref[mm, :, sl] + pv[:, sl]
            m_ref[mm] = m_new

    n_tiles = (qi + 1) * ratio
    n_far = jnp.maximum(n_tiles - n_near, 0)
    far_bias = rb_ref[N_BUCKETS - 1, h]
    scores(0, 0)

    def body(j, carry):
        slot = j & 1
        softmax_pv(j, slot, far_bias, True)
        scores(j + 1, 1 - slot)
        return carry

    lax.fori_loop(0, n_far, body, 0)

    for t in range(n_near):
        j = n_tiles - n_near + t

        @pl.when(j >= 0)
        def _():
            slot = j & 1
            softmax_pv(j, slot, nb_ref[0, t], False)
            if t < n_near - 1:
                scores(j + 1, 1 - slot)

    lam = (jnp.exp(jnp.sum(lq1_ref[...] * lk1_ref[...], axis=1, keepdims=True))
           - jnp.exp(jnp.sum(lq2_ref[...] * lk2_ref[...], axis=1, keepdims=True))
           + lambda_init)
    o = (acc_ref[0, :, 0:LANES] / acc_ref[0, :, LANES:2 * LANES]
         - lam * (acc_ref[1, :, 0:LANES] / acc_ref[1, :, LANES:2 * LANES]))
    ms = jnp.mean(o * o, axis=-1, keepdims=True)
    o = o * lax.rsqrt(ms + EPS) * sub_ref[...] * (1.0 - lambda_init)
    o_ref[...] = o.astype(o_ref.dtype)


def _diff_attention(qkv, rel_bias, lq1, lk1, lq2, lk2, subln, lambda_init, batch, seq,
                    tq=512, tk=512):
    tq = min(tq, seq)
    tk = min(tk, tq)
    assert tk >= MAX_DISTANCE
    n = qkv.shape[0]
    nq = seq // tq
    nh = DIFF_HEADS
    n_near = tq // tk + 1
    near = _near_bias(rel_bias, tq, tk, n_near)
    vec = lambda a: a.reshape(1, -1)
    cvec = pl.BlockSpec((1, HEAD_DIM), lambda b, h, i: (0, 0))
    return pl.pallas_call(
        functools.partial(_diff_kernel, tq=tq, tk=tk, n_near=n_near, lambda_init=lambda_init),
        grid=(batch, nh, nq),
        in_specs=[
            pl.BlockSpec(memory_space=pltpu.SMEM),
            pl.BlockSpec((tq, LANES), lambda b, h, i: (b * nq + i, h)),
            pl.BlockSpec((seq, LANES), lambda b, h, i: (b, nh + h)),
            pl.BlockSpec((seq, LANES), lambda b, h, i: (b, 2 * nh + h)),
            pl.BlockSpec((1, n_near, tq, tk), lambda b, h, i: (h, 0, 0, 0)),
            cvec, cvec, cvec, cvec,
            pl.BlockSpec((1, LANES), lambda b, h, i: (0, 0)),
        ],
        out_specs=pl.BlockSpec((tq, LANES), lambda b, h, i: (b * nq + i, h)),
        out_shape=jax.ShapeDtypeStruct((n, DIFF_V), BF16),
        scratch_shapes=[
            pltpu.VMEM((2, tq, LANES), F32),
            pltpu.VMEM((2, tq, 2 * LANES), F32),
            pltpu.VMEM((seq, 2 * LANES), BF16),
            pltpu.VMEM((2, 2, tq, tk), F32),
        ],
        compiler_params=_cparams(("arbitrary", "arbitrary", "arbitrary")),
        name="diff_attention",
    )(rel_bias, qkv, qkv, qkv, near, vec(lq1), vec(lk1), vec(lq2), vec(lk2), vec(subln))


def _moe_kernel(x_ref, lnw_ref, sh_ref, sc_ref, g_ref, rw_ref, rb_ref, lnf_ref,
                wg_ref, wu_ref, wd_ref, o_ref, h_ref, gates_ref, acc_ref, y_ref, *, tm):
    e = pl.program_id(1)
    k = pl.program_id(2)
    last_k = pl.num_programs(2) - 1
    lane = lax.broadcasted_iota(jnp.int32, (tm, LANES), 1)

    @pl.when((e == 0) & (k == 0))
    def _():
        h = _norm_mod(x_ref[...], lnw_ref[...], sh_ref[0], sc_ref[0])
        h_ref[...] = h.astype(BF16)
        hh, hl = _split_bf16(h)
        wh, wl = _split_bf16(rw_ref[...])
        logits = _dot(hh, wh) + _dot(hl, wh) + _dot(hh, wl) + rb_ref[...]
        neg_inf = jnp.float32(-jnp.inf)
        lanef = lane.astype(F32)
        logits = jnp.where(lane < N_EXPERTS, logits, neg_inf)
        m1 = jnp.max(logits, axis=1, keepdims=True)
        i1 = jnp.min(jnp.where(logits == m1, lanef, float(LANES)), axis=1, keepdims=True)
        rest = jnp.where(lanef == i1, neg_inf, logits)
        m2 = jnp.max(rest, axis=1, keepdims=True)
        i2 = jnp.min(jnp.where(rest == m2, lanef, float(LANES)), axis=1, keepdims=True)
        w1 = 1.0 / (1.0 + jnp.exp(m2 - m1))
        w2 = 1.0 - w1
        gates_ref[...] = jnp.where(lanef == i1, w1, 0.0) + jnp.where(lanef == i2, w2, 0.0)
        y_ref[...] = jnp.zeros_like(y_ref)

    @pl.when(k == 0)
    def _():
        acc_ref[...] = jnp.zeros_like(acc_ref)

    h = h_ref[...]
    gt = _dot(h, wg_ref[0])
    up = _dot(h, wu_ref[0])
    act = (gt * _sigmoid(gt) * up).astype(BF16)
    acc_ref[...] += _dot(act, wd_ref[0])

    @pl.when(k == last_k)
    def _():
        ge = jnp.sum(jnp.where(lane == e, gates_ref[...], 0.0), axis=1, keepdims=True)
        y_ref[...] += ge * acc_ref[...]

    @pl.when((e == pl.num_programs(1) - 1) & (k == last_k))
    def _():
        xo = x_ref[...] + g_ref[0] * y_ref[...]
        ms = jnp.mean(xo * xo, axis=-1, keepdims=True)
        o_ref[...] = xo * lax.rsqrt(ms + EPS) * lnf_ref[...]


def _moe(x, lnw, shift, scale, gate, router_w, router_b, ln_final, wg, wu, wd, seq,
         tm=512, tf=896):
    n, d = x.shape
    ne, _, f = wg.shape
    tiles_per_seq = seq // tm
    rw = jnp.zeros((d, LANES), F32).at[:, :ne].set(router_w)
    rb = jnp.zeros((1, LANES), F32).at[0, :ne].set(router_b)
    bidx = lambda i, e, k: (i // tiles_per_seq, 0, 0)
    const = lambda i, e, k: (0, 0)
    return pl.pallas_call(
        functools.partial(_moe_kernel, tm=tm),
        grid=(n // tm, ne, f // tf),
        in_specs=[
            pl.BlockSpec((tm, d), lambda i, e, k: (i, 0)),
            pl.BlockSpec((1, d), const),
            pl.BlockSpec((1, 1, d), bidx),
            pl.BlockSpec((1, 1, d), bidx),
            pl.BlockSpec((1, 1, d), bidx),
            pl.BlockSpec((d, LANES), const),
            pl.BlockSpec((1, LANES), const),
            pl.BlockSpec((1, d), const),
            pl.BlockSpec((1, d, tf), lambda i, e, k: (e, 0, k)),
            pl.BlockSpec((1, d, tf), lambda i, e, k: (e, 0, k)),
            pl.BlockSpec((1, tf, d), lambda i, e, k: (e, k, 0)),
        ],
        out_specs=pl.BlockSpec((tm, d), lambda i, e, k: (i, 0)),
        out_shape=jax.ShapeDtypeStruct((n, d), F32),
        scratch_shapes=[
            pltpu.VMEM((tm, d), BF16),
            pltpu.VMEM((tm, LANES), F32),
            pltpu.VMEM((tm, d), F32),
            pltpu.VMEM((tm, d), F32),
        ],
        compiler_params=_cparams(("arbitrary", "arbitrary", "arbitrary")),
        name="moe_ffn",
    )(x, lnw, shift, scale, gate, rw, rb, ln_final, wg, wu, wd)


def kernel(x, c, rel_bias, ada_w, ada_b, ln_mix, ln_ffn, ln_final, even_w_in, even_w_out, lru_conv_w, lru_conv_b, lru_gate_a_w, lru_gate_a_b, lru_gate_x_w, lru_gate_x_b, lru_lambda, ffn_w_gate, ffn_w_up, ffn_w_down, odd_w_in, odd_w_out, diff_lambda_q1, diff_lambda_k1, diff_lambda_q2, diff_lambda_k2, diff_subln, router_w, router_b, moe_w_gate, moe_w_up, moe_w_down):
    batch, seq, d = x.shape
    n = batch * seq
    xf = x.reshape(n, d)
    mod = _adaln(c, ada_w, ada_b)
    part = lambda layer, j: mod[layer, :, j * d:(j + 1) * d].reshape(batch, 1, d)
    row = lambda v: v.reshape(1, -1)

    sh1, sc1, g1, sh2, sc2, g2 = (part(0, j) for j in range(6))
    qkv_w = 3 * SB_WIDTH
    qkv, xg = _inproj(xf, row(ln_mix[0]), sh1, sc1, even_w_in[0].astype(BF16),
                      ((0, qkv_w), (qkv_w, qkv_w + 2 * LRU_WIDTH)), (BF16, F32), seq)
    ya = _sb_attention(qkv, batch, seq)
    yb = _rg_lru(xg, lru_conv_w[0], lru_conv_b[0], lru_gate_a_w[0], lru_gate_a_b[0],
                 lru_gate_x_w[0], lru_gate_x_b[0], lru_lambda[0], batch, seq)
    xf = _outproj(xf, g1, even_w_out[0].astype(BF16), (ya, yb), seq)
    xf = _ffn(xf, row(ln_ffn[0]), sh2, sc2, g2, ffn_w_gate[0].astype(BF16),
              ffn_w_up[0].astype(BF16), ffn_w_down[0].astype(BF16), seq)

    sh1, sc1, g1, sh2, sc2, g2 = (part(1, j) for j in range(6))
    lambda_init = 0.8 - 0.6 * math.exp(-0.3 * 1)
    (qkv,) = _inproj(xf, row(ln_mix[1]), sh1, sc1, odd_w_in[0].astype(BF16),
                     ((0, 2 * DIFF_QK + DIFF_V),), (BF16,), seq)
    yo = _diff_attention(qkv, rel_bias, diff_lambda_q1[0], diff_lambda_k1[0], diff_lambda_q2[0],
                         diff_lambda_k2[0], diff_subln[0], lambda_init, batch, seq)
    xf = _outproj(xf, g1, odd_w_out[0].astype(BF16), (yo,), seq)
    out = _moe(xf, row(ln_ffn[1]), sh2, sc2, g2, router_w[0], router_b[0], row(ln_final),
               moe_w_gate[0].astype(BF16), moe_w_up[0].astype(BF16), moe_w_down[0].astype(BF16), seq)
    return out.reshape(batch, seq, d)
```

```python
import functools
import math

import jax
import jax.numpy as jnp
from jax import lax
from jax.experimental import pallas as pl
from jax.experimental.pallas import tpu as pltpu

F32 = jnp.float32
BF16 = jnp.bfloat16

D_MODEL = 1024
HEAD_DIM = 64
LANES = 128
SLAB = D_MODEL // LANES
SB_HEADS = 8
SB_WIDTH = SB_HEADS * HEAD_DIM
LRU_WIDTH = 512
LRU_BLOCKS = 8
CONV_WIDTH = 4
LRU_C = 8.0
DIFF_HEADS = 8
DIFF_QK = 2 * DIFF_HEADS * HEAD_DIM
DIFF_V = DIFF_HEADS * 2 * HEAD_DIM
N_BUCKETS = 32
MAX_EXACT = N_BUCKETS // 2
MAX_DISTANCE = 128
D_FF_DENSE = 2816
N_EXPERTS = 8
D_FF_EXPERT = 3584
EPS = 1e-6
NEG_BIG = -1e30
UNDERFLOW_LOG = 110.0

VMEM_LIMIT = 56 * 1024 * 1024


def _cparams(sem):
    return pltpu.CompilerParams(dimension_semantics=sem, vmem_limit_bytes=VMEM_LIMIT)


def _dot(a, b):
    return jnp.dot(a, b, preferred_element_type=F32)


def _dot_nt(a, b):
    return lax.dot_general(a, b, (((1,), (1,)), ((), ())), preferred_element_type=F32)


def _split_bf16(x):
    hi = x.astype(BF16)
    lo = (x - hi.astype(F32)).astype(BF16)
    return hi, lo


def _sigmoid(x):
    return 1.0 / (1.0 + jnp.exp(-x))


def _norm_mod(x, lnw, shift, scale):
    ms = jnp.mean(x * x, axis=-1, keepdims=True)
    n = x * lax.rsqrt(ms + EPS)
    return (n * lnw) * (1.0 + scale) + shift


def _adaln_kernel(c_ref, w_ref, b_ref, o_ref):
    c = c_ref[...]
    cond = c * _sigmoid(c)
    ch, cl = _split_bf16(cond)
    wh, wl = _split_bf16(w_ref[0])
    o_ref[0] = _dot(ch, wh) + _dot(ch, wl) + _dot(cl, wh) + b_ref[0]


def _adaln(c, ada_w, ada_b):
    depth, d, n6 = ada_w.shape
    b = c.shape[0]
    tn = 1536
    return pl.pallas_call(
        _adaln_kernel,
        grid=(depth, n6 // tn),
        in_specs=[
            pl.BlockSpec((b, d), lambda l, j: (0, 0)),
            pl.BlockSpec((1, d, tn), lambda l, j: (l, 0, j)),
            pl.BlockSpec((1, 1, tn), lambda l, j: (l, 0, j)),
        ],
        out_specs=pl.BlockSpec((1, b, tn), lambda l, j: (l, 0, j)),
        out_shape=jax.ShapeDtypeStruct((depth, b, n6), F32),
        compiler_params=_cparams(("arbitrary", "arbitrary")),
        name="adaln",
    )(c, ada_w, ada_b.reshape(depth, 1, n6))


def _inproj_kernel(x_ref, lnw_ref, sh_ref, sc_ref, w_ref, *o_refs, splits):
    h = _norm_mod(x_ref[...], lnw_ref[...], sh_ref[0], sc_ref[0]).astype(BF16)
    for o_ref, (lo, hi) in zip(o_refs, splits):
        o_ref[...] = _dot(h, w_ref[:, lo:hi]).astype(o_ref.dtype)


def _inproj(x, lnw, shift, scale, w, splits, dtypes, seq, tm=512):
    n, d = x.shape
    tiles_per_seq = seq // tm
    bidx = lambda i: (i // tiles_per_seq, 0, 0)
    return pl.pallas_call(
        functools.partial(_inproj_kernel, splits=splits),
        grid=(n // tm,),
        in_specs=[
            pl.BlockSpec((tm, d), lambda i: (i, 0)),
            pl.BlockSpec((1, d), lambda i: (0, 0)),
            pl.BlockSpec((1, 1, d), bidx),
            pl.BlockSpec((1, 1, d), bidx),
            pl.BlockSpec(w.shape, lambda i: (0, 0)),
        ],
        out_specs=[pl.BlockSpec((tm, hi - lo), lambda i: (i, 0)) for lo, hi in splits],
        out_shape=[jax.ShapeDtypeStruct((n, hi - lo), dt) for (lo, hi), dt in zip(splits, dtypes)],
        compiler_params=_cparams(("arbitrary",)),
        name="inproj",
    )(x, lnw, shift, scale, w)


def _sb_kernel(q_ref, k_ref, v_ref, o_ref, acc_ref, c_ref, *, tq, tk):
    qi = pl.program_id(2)
    ratio = tq // tk
    n_tiles = (qi + 1) * ratio
    nc = tk // LANES
    lane = lax.broadcasted_iota(jnp.int32, (tq, LANES), 1)
    q = q_ref[...] * (HEAD_DIM ** -0.5)
    zero = jnp.zeros_like(q)
    qs = (jnp.where(lane < HEAD_DIM, q, zero), jnp.where(lane >= HEAD_DIM, q, zero))
    rj = lax.broadcasted_iota(jnp.int32, (tk, tk), 0)
    cs = lax.broadcasted_iota(jnp.int32, (tk, tk), 1)
    neg_upper = jnp.where(rj > cs, -1.0, 0.0).astype(BF16)

    acc_ref[...] = jnp.zeros_like(acc_ref)
    c_ref[...] = jnp.zeros_like(c_ref)

    def causal_mask(j):
        q_pos = qi * tq + lax.broadcasted_iota(jnp.int32, (tq, tk), 0)
        k_pos = j * tk + lax.broadcasted_iota(jnp.int32, (tq, tk), 1)
        return k_pos < q_pos

    def tile(j, masked):
        start = pl.multiple_of(j * tk, tk)
        ks = k_ref[pl.ds(start, tk), :]
        vs = v_ref[pl.ds(start, tk), :]
        for hh in range(2):
            z = _dot_nt(qs[hh], ks)
            sp = jnp.maximum(z, 0.0) + jnp.log(1.0 + jnp.exp(-jnp.abs(z)))
            if masked:
                sp = jnp.where(causal_mask(j), sp, 0.0)
            hi, lo = _split_bf16(sp)
            later = _dot(hi, neg_upper) + _dot(lo, neg_upper)
            c = c_ref[hh]
            w = jnp.concatenate(
                [jnp.exp((z - sp)[:, i * LANES:(i + 1) * LANES]
                         + later[:, i * LANES:(i + 1) * LANES] - c) for i in range(nc)], axis=1)
            if masked:
                w = jnp.where(causal_mask(j), w, 0.0)
            acc_ref[hh] += _dot(w.astype(BF16), vs)
            c_ref[hh] = c + jnp.broadcast_to(jnp.sum(sp, axis=1, keepdims=True), (tq, LANES))

    for d in range(ratio):
        tile(n_tiles - 1 - d, True)

    def cond(state):
        n, c_min = state
        return (n < n_tiles - ratio) & (c_min < UNDERFLOW_LOG)

    def body(state):
        n, _ = state
        tile(n_tiles - ratio - 1 - n, False)
        return n + 1, jnp.min(c_ref[...])

    lax.while_loop(cond, body, (jnp.int32(0), jnp.min(c_ref[...])))

    o_ref[...] = jnp.where(lane < HEAD_DIM, acc_ref[0], acc_ref[1]).astype(o_ref.dtype)


def _sb_attention(qkv, batch, seq, tq=512, tk=256):
    tq = min(tq, seq)
    tk = min(tk, tq)
    n = qkv.shape[0]
    pairs = SB_WIDTH // LANES
    nq = seq // tq
    return pl.pallas_call(
        functools.partial(_sb_kernel, tq=tq, tk=tk),
        grid=(batch, pairs, nq),
        in_specs=[
            pl.BlockSpec((tq, LANES), lambda b, p, i: (b * nq + i, p)),
            pl.BlockSpec((seq, LANES), lambda b, p, i: (b, pairs + p)),
            pl.BlockSpec((seq, LANES), lambda b, p, i: (b, 2 * pairs + p)),
        ],
        out_specs=pl.BlockSpec((tq, LANES), lambda b, p, i: (b * nq + i, p)),
        out_shape=jax.ShapeDtypeStruct((n, SB_WIDTH), BF16),
        scratch_shapes=[
            pltpu.VMEM((2, tq, LANES), F32),
            pltpu.VMEM((2, tq, LANES), F32),
        ],
        compiler_params=_cparams(("arbitrary", "arbitrary", "arbitrary")),
        name="sb_attention",
    )(qkv, qkv, qkv)


def _lru_kernel(xg_ref, cw_ref, cb_ref, wg_ref, bg_ref, lam_ref, o_ref,
                ext_ref, a_ref, u_ref, h_ref, carry_ref, *, tt):
    c = LRU_WIDTH
    halo = 8

    @pl.when(pl.program_id(1) == 0)
    def _():
        ext_ref[0:halo, :] = jnp.zeros((halo, c), F32)
        carry_ref[...] = jnp.zeros_like(carry_ref)

    xb = xg_ref[:, 0:c]
    gb = xg_ref[:, c:2 * c]
    ext_ref[halo:halo + tt, :] = xb
    xc = cb_ref[...] + cw_ref[CONV_WIDTH - 1:CONV_WIDTH, :] * xb
    for i in range(CONV_WIDTH - 1):
        off = halo - (CONV_WIDTH - 1) + i
        xc = xc + cw_ref[i:i + 1, :] * ext_ref[off:off + tt, :]
    ext_ref[0:halo, :] = xb[tt - halo:tt, :]

    gates = _dot(xc.astype(BF16), wg_ref[...]) + bg_ref[...]
    r = _sigmoid(gates[:, 0:c])
    ig = _sigmoid(gates[:, c:2 * c])
    lam = lam_ref[...]
    log_sig_lam = jnp.minimum(lam, 0.0) - jnp.log(1.0 + jnp.exp(-jnp.abs(lam)))
    a = jnp.exp(LRU_C * r * log_sig_lam)
    u = jnp.sqrt(1.0 - a * a) * (ig * xc)

    r8 = lax.broadcasted_iota(jnp.int32, (tt, c), 0) & 7
    for s in (1, 2, 4):
        a_prev = pltpu.roll(a, s, axis=0)
        u_prev = pltpu.roll(u, s, axis=0)
        valid = r8 >= s
        u = jnp.where(valid, a * u_prev + u, u)
        a = jnp.where(valid, a * a_prev, a)
    a_ref[...] = a
    u_ref[...] = u

    def body(g, carry):
        off = pl.multiple_of(g * 8, 8)
        h = a_ref[pl.ds(off, 8), :] * carry + u_ref[pl.ds(off, 8), :]
        h_ref[pl.ds(off, 8), :] = h
        return jnp.broadcast_to(h[7:8, :], (8, c))

    carry_ref[...] = lax.fori_loop(0, tt // 8, body, carry_ref[...])

    gelu = 0.5 * gb * (1.0 + jnp.tanh(math.sqrt(2.0 / math.pi) * (gb + 0.044715 * (gb * gb * gb))))
    o_ref[...] = (h_ref[...] * gelu).astype(o_ref.dtype)


def _block_diag(w):
    g, bi, bo = w.shape
    eye = jnp.eye(g, dtype=w.dtype)
    return (w[:, :, None, :] * eye[:, None, :, None]).reshape(g * bi, g * bo)


def _rg_lru(xg, conv_w, conv_b, ga_w, ga_b, gx_w, gx_b, lam, batch, seq, tt=512):
    tt = min(tt, seq)
    n = xg.shape[0]
    c = LRU_WIDTH
    nt = seq // tt
    wg = jnp.concatenate([_block_diag(ga_w), _block_diag(gx_w)], axis=1).astype(BF16)
    bg = jnp.concatenate([ga_b, gx_b]).reshape(1, 2 * c)
    const = lambda b, t: (0, 0)
    return pl.pallas_call(
        functools.partial(_lru_kernel, tt=tt),
        grid=(batch, nt),
        in_specs=[
            pl.BlockSpec((tt, 2 * c), lambda b, t: (b * nt + t, 0)),
            pl.BlockSpec((CONV_WIDTH, c), const),
            pl.BlockSpec((1, c), const),
            pl.BlockSpec((c, 2 * c), const),
            pl.BlockSpec((1, 2 * c), const),
            pl.BlockSpec((1, c), const),
        ],
        out_specs=pl.BlockSpec((tt, c), lambda b, t: (b * nt + t, 0)),
        out_shape=jax.ShapeDtypeStruct((n, c), BF16),
        scratch_shapes=[
            pltpu.VMEM((tt + 8, c), F32),
            pltpu.VMEM((tt, c), F32),
            pltpu.VMEM((tt, c), F32),
            pltpu.VMEM((tt, c), F32),
            pltpu.VMEM((8, c), F32),
        ],
        compiler_params=_cparams(("arbitrary", "arbitrary")),
        name="rg_lru",
    )(xg, conv_w, conv_b.reshape(1, c), wg, bg, lam.reshape(1, c))


def _outproj_kernel(*refs, widths):
    x_ref, g_ref, w_ref = refs[0], refs[1], refs[2]
    y_refs = refs[3:3 + len(widths)]
    o_ref = refs[3 + len(widths)]
    mix = None
    off = 0
    for y_ref, wd in zip(y_refs, widths):
        part = _dot(y_ref[...], w_ref[off:off + wd, :])
        mix = part if mix is None else mix + part
        off += wd
    o_ref[...] = x_ref[...] + g_ref[0] * mix


def _outproj(x, gate, w, ys, seq, tm=512):
    n, d = x.shape
    tiles_per_seq = seq // tm
    widths = tuple(y.shape[1] for y in ys)
    return pl.pallas_call(
        functools.partial(_outproj_kernel, widths=widths),
        grid=(n // tm,),
        in_specs=[
            pl.BlockSpec((tm, d), lambda i: (i, 0)),
            pl.BlockSpec((1, 1, d), lambda i: (i // tiles_per_seq, 0, 0)),
            pl.BlockSpec(w.shape, lambda i: (0, 0)),
        ] + [pl.BlockSpec((tm, wd), lambda i: (i, 0)) for wd in widths],
        out_specs=pl.BlockSpec((tm, d), lambda i: (i, 0)),
        out_shape=jax.ShapeDtypeStruct((n, d), F32),
        compiler_params=_cparams(("arbitrary",)),
        name="outproj",
    )(x, gate, w, *ys)


def _ffn_kernel(x_ref, lnw_ref, sh_ref, sc_ref, g_ref, wg_ref, wu_ref, wd_ref, o_ref,
                h_ref, acc_ref):
    k = pl.program_id(1)

    @pl.when(k == 0)
    def _():
        h_ref[...] = _norm_mod(x_ref[...], lnw_ref[...], sh_ref[0], sc_ref[0]).astype(BF16)
        acc_ref[...] = jnp.zeros_like(acc_ref)

    h = h_ref[...]
    gt = _dot(h, wg_ref[...])
    up = _dot(h, wu_ref[...])
    act = (gt * _sigmoid(gt) * up).astype(BF16)
    acc_ref[...] += _dot(act, wd_ref[...])

    @pl.when(k == pl.num_programs(1) - 1)
    def _():
        o_ref[...] = x_ref[...] + g_ref[0] * acc_ref[...]


def _ffn(x, lnw, shift, scale, gate, wg, wu, wd, seq, tm=512, tf=1408):
    n, d = x.shape
    f = wg.shape[1]
    tiles_per_seq = seq // tm
    bidx = lambda i, k: (i // tiles_per_seq, 0, 0)
    return pl.pallas_call(
        _ffn_kernel,
        grid=(n // tm, f // tf),
        in_specs=[
            pl.BlockSpec((tm, d), lambda i, k: (i, 0)),
            pl.BlockSpec((1, d), lambda i, k: (0, 0)),
            pl.BlockSpec((1, 1, d), bidx),
            pl.BlockSpec((1, 1, d), bidx),
            pl.BlockSpec((1, 1, d), bidx),
            pl.BlockSpec((d, tf), lambda i, k: (0, k)),
            pl.BlockSpec((d, tf), lambda i, k: (0, k)),
            pl.BlockSpec((tf, d), lambda i, k: (k, 0)),
        ],
        out_specs=pl.BlockSpec((tm, d), lambda i, k: (i, 0)),
        out_shape=jax.ShapeDtypeStruct((n, d), F32),
        scratch_shapes=[pltpu.VMEM((tm, d), BF16), pltpu.VMEM((tm, d), F32)],
        compiler_params=_cparams(("arbitrary", "arbitrary")),
        name="ffn_dense",
    )(x, lnw, shift, scale, gate, wg, wu, wd)


def _bias_kernel(rb_ref, o_ref, *, tq, tk, n_near):
    h = pl.program_id(0)
    t = pl.program_id(1)
    row = lax.broadcasted_iota(jnp.int32, (tq, tk), 0)
    col = lax.broadcasted_iota(jnp.int32, (tq, tk), 1)
    rel = row - col + (n_near - 1 - t) * tk - (tq - tk)
    nn = jnp.maximum(rel, 0)
    nf = jnp.maximum(nn, 1).astype(F32)
    large = MAX_EXACT + (jnp.log(nf / MAX_EXACT) / math.log(MAX_DISTANCE / MAX_EXACT)
                         * (N_BUCKETS - MAX_EXACT)).astype(jnp.int32)
    large = jnp.minimum(large, N_BUCKETS - 1)
    bucket = jnp.where(nn < MAX_EXACT, nn, large)
    bias = jnp.zeros((tq, tk), F32)
    for b in range(N_BUCKETS):
        bias = jnp.where(bucket == b, rb_ref[b, h], bias)
    o_ref[0, 0] = jnp.where(rel >= 0, bias, NEG_BIG)


def _near_bias(rel_bias, tq, tk, n_near):
    return pl.pallas_call(
        functools.partial(_bias_kernel, tq=tq, tk=tk, n_near=n_near),
        grid=(DIFF_HEADS, n_near),
        in_specs=[pl.BlockSpec(memory_space=pltpu.SMEM)],
        out_specs=pl.BlockSpec((1, 1, tq, tk), lambda h, t: (h, t, 0, 0)),
        out_shape=jax.ShapeDtypeStruct((DIFF_HEADS, n_near, tq, tk), F32),
        compiler_params=_cparams(("arbitrary", "arbitrary")),
        name="t5_bias_tiles",
    )(rel_bias)


def _diff_kernel(rb_ref, q_ref, k_ref, v_ref, nb_ref, lq1_ref, lk1_ref, lq2_ref, lk2_ref,
                 sub_ref, o_ref, m_ref, acc_ref, vext_ref, z_ref, *, tq, tk, n_near,
                 lambda_init):
    h = pl.program_id(1)
    qi = pl.program_id(2)
    ratio = tq // tk
    lane = lax.broadcasted_iota(jnp.int32, (tq, LANES), 1)
    q = q_ref[...] * (HEAD_DIM ** -0.5)
    zero = jnp.zeros_like(q)
    qs = (jnp.where(lane < HEAD_DIM, q, zero), jnp.where(lane >= HEAD_DIM, q, zero))

    @pl.when(qi == 0)
    def _():
        vext_ref[:, 0:LANES] = v_ref[...]
        vext_ref[:, LANES:2 * LANES] = jnp.ones(v_ref.shape, BF16)

    m_ref[...] = jnp.full_like(m_ref, NEG_BIG)
    acc_ref[...] = jnp.zeros_like(acc_ref)
    nc = tk // LANES

    def scores(j, slot):
        start = pl.multiple_of(j * tk, tk)
        ks = k_ref[pl.ds(start, tk), :]
        for mm in range(2):
            z_ref[slot, mm] = _dot_nt(qs[mm], ks)

    def softmax_pv(j, slot, bias, bias_is_scalar):
        start = pl.multiple_of(j * tk, tk)
        vs = vext_ref[pl.ds(start, tk), :]
        for mm in range(2):
            zc = [z_ref[slot, mm, :, c * LANES:(c + 1) * LANES] for c in range(nc)]
            if not bias_is_scalar:
                zc = [zc[c] + bias[:, c * LANES:(c + 1) * LANES] for c in range(nc)]
            part = functools.reduce(jnp.maximum, zc)
            rm = jnp.broadcast_to(jnp.max(part, axis=1, keepdims=True), (tq, LANES))
            m_old = m_ref[mm]
            if bias_is_scalar:
                m_new = jnp.maximum(m_old, rm + bias)
                shift = m_new - bias
            else:
                m_new = jnp.maximum(m_old, rm)
                shift = m_new
            alpha = jnp.exp(m_old - m_new)
            p = jnp.concatenate([jnp.exp(z - shift) for z in zc], axis=1).astype(BF16)
            pv = _dot(p, vs)
            for c in range(2):
                sl = slice(c * LANES, (c + 1) * LANES)
                acc_ref[mm, :, sl] = alpha * acc_ref[mm, :, sl] + pv[:, sl]
            m_ref[mm] = m_new

    n_tiles = (qi + 1) * ratio
    n_far = jnp.maximum(n_tiles - n_near, 0)
    far_bias = rb_ref[N_BUCKETS - 1, h]
    scores(0, 0)

    def body(j, carry):
        slot = j & 1
        softmax_pv(j, slot, far_bias, True)
        scores(j + 1, 1 - slot)
        return carry

    lax.fori_loop(0, n_far, body, 0)

    for t in range(n_near):
        j = n_tiles - n_near + t

        @pl.when(j >= 0)
        def _():
            slot = j & 1
            softmax_pv(j, slot, nb_ref[0, t], False)
            if t < n_near - 1:
                scores(j + 1, 1 - slot)

    lam = (jnp.exp(jnp.sum(lq1_ref[...] * lk1_ref[...], axis=1, keepdims=True))
           - jnp.exp(jnp.sum(lq2_ref[...] * lk2_ref[...], axis=1, keepdims=True))
           + lambda_init)
    o = (acc_ref[0, :, 0:LANES] / acc_ref[0, :, LANES:2 * LANES]
         - lam * (acc_ref[1, :, 0:LANES] / acc_ref[1, :, LANES:2 * LANES]))
    ms = jnp.mean(o * o, axis=-1, keepdims=True)
    o = o * lax.rsqrt(ms + EPS) * sub_ref[...] * (1.0 - lambda_init)
    o_ref[...] = o.astype(o_ref.dtype)


def _diff_attention(qkv, rel_bias, lq1, lk1, lq2, lk2, subln, lambda_init, batch, seq,
                    tq=512, tk=512):
    tq = min(tq, seq)
    tk = min(tk, tq)
    assert tk >= MAX_DISTANCE
    n = qkv.shape[0]
    nq = seq // tq
    nh = DIFF_HEADS
    n_near = tq // tk + 1
    near = _near_bias(rel_bias, tq, tk, n_near)
    vec = lambda a: a.reshape(1, -1)
    cvec = pl.BlockSpec((1, HEAD_DIM), lambda b, h, i: (0, 0))
    return pl.pallas_call(
        functools.partial(_diff_kernel, tq=tq, tk=tk, n_near=n_near, lambda_init=lambda_init),
        grid=(batch, nh, nq),
        in_specs=[
            pl.BlockSpec(memory_space=pltpu.SMEM),
            pl.BlockSpec((tq, LANES), lambda b, h, i: (b * nq + i, h)),
            pl.BlockSpec((seq, LANES), lambda b, h, i: (b, nh + h)),
            pl.BlockSpec((seq, LANES), lambda b, h, i: (b, 2 * nh + h)),
            pl.BlockSpec((1, n_near, tq, tk), lambda b, h, i: (h, 0, 0, 0)),
            cvec, cvec, cvec, cvec,
            pl.BlockSpec((1, LANES), lambda b, h, i: (0, 0)),
        ],
        out_specs=pl.BlockSpec((tq, LANES), lambda b, h, i: (b * nq + i, h)),
        out_shape=jax.ShapeDtypeStruct((n, DIFF_V), BF16),
        scratch_shapes=[
            pltpu.VMEM((2, tq, LANES), F32),
            pltpu.VMEM((2, tq, 2 * LANES), F32),
            pltpu.VMEM((seq, 2 * LANES), BF16),
            pltpu.VMEM((2, 2, tq, tk), F32),
        ],
        compiler_params=_cparams(("arbitrary", "arbitrary", "arbitrary")),
        name="diff_attention",
    )(rel_bias, qkv, qkv, qkv, near, vec(lq1), vec(lk1), vec(lq2), vec(lk2), vec(subln))


def _to_slabs(ref, value):
    rows = value.shape[0]
    for j in range(SLAB):
        ref[pl.ds(j, rows, stride=SLAB), :] = value[:, j * LANES:(j + 1) * LANES]


def _from_slabs(ref, rows):
    return jnp.concatenate([ref[pl.ds(j, rows, stride=SLAB), :] for j in range(SLAB)], axis=1)


def _router_kernel(x_ref, lnw_ref, sh_ref, sc_ref, rw_ref, rb_ref,
                   hs_ref, info_ref, totals_ref, run_ref, *, tm):
    @pl.when(pl.program_id(0) == 0)
    def _():
        run_ref[...] = jnp.zeros_like(run_ref)

    lane = lax.broadcasted_iota(jnp.int32, (tm, LANES), 1)
    h = _norm_mod(x_ref[...], lnw_ref[...], sh_ref[0], sc_ref[0])
    _to_slabs(hs_ref, h)
    hh, hl = _split_bf16(h)
    wh, wl = _split_bf16(rw_ref[...])
    logits = _dot(hh, wh) + _dot(hl, wh) + _dot(hh, wl) + rb_ref[...]
    neg_inf = jnp.float32(-jnp.inf)
    lanef = lane.astype(F32)
    logits = jnp.where(lane < N_EXPERTS, logits, neg_inf)
    m1 = jnp.max(logits, axis=1, keepdims=True)
    i1 = jnp.min(jnp.where(logits == m1, lanef, float(LANES)), axis=1, keepdims=True)
    rest = jnp.where(lanef == i1, neg_inf, logits)
    m2 = jnp.max(rest, axis=1, keepdims=True)
    i2 = jnp.min(jnp.where(rest == m2, lanef, float(LANES)), axis=1, keepdims=True)
    w1 = 1.0 / (1.0 + jnp.exp(m2 - m1))
    w2 = 1.0 - w1
    picked_f = jnp.where((lanef == i1) | (lanef == i2), 1.0, 0.0)
    ri = lax.broadcasted_iota(jnp.int32, (tm, tm), 0)
    ci = lax.broadcasted_iota(jnp.int32, (tm, tm), 1)
    earlier = jnp.where(ci < ri, 1.0, 0.0).astype(BF16)
    rank = _dot(earlier, picked_f.astype(BF16)) + run_ref[...]
    pick = lambda idx: jnp.sum(jnp.where(lanef == idx, rank, 0.0), axis=1, keepdims=True)
    r1, r2 = pick(i1), pick(i2)
    cols = (i1, i2, w1, w2, r1, r2)
    info = jnp.zeros((tm, LANES), F32)
    for c, v in enumerate(cols):
        info = jnp.where(lane == c, v, info)
    info_ref[...] = info
    run_ref[...] += jnp.sum(picked_f, axis=0, keepdims=True)
    totals_ref[...] = run_ref[...]


def _dispatch_kernel(p1_ref, p2_ref, hs_ref, zero_ref, xs_ref, sem, *, tm):
    del zero_ref
    base = pl.program_id(0) * tm

    def row_copy(t, p_ref):
        src = pl.multiple_of((base + t) * SLAB, SLAB)
        dst = pl.multiple_of(p_ref[t] * SLAB, SLAB)
        return pltpu.make_async_copy(hs_ref.at[pl.ds(src, SLAB)], xs_ref.at[pl.ds(dst, SLAB)], sem)

    def issue(t, carry):
        row_copy(t, p1_ref).start()
        row_copy(t, p2_ref).start()
        return carry

    lax.fori_loop(0, tm, issue, 0, unroll=8)

    def drain(t, carry):
        row_copy(t, p1_ref).wait()
        row_copy(t, p2_ref).wait()
        return carry

    lax.fori_loop(0, tm, drain, 0, unroll=8)


def _expert_kernel(te_ref, nu_ref, xs_ref, wg_ref, wu_ref, wd_ref, os_ref, xc_ref, acc_ref, *, tm):
    j = pl.program_id(0)
    k = pl.program_id(1)

    @pl.when(j < nu_ref[0])
    def _():
        @pl.when(k == 0)
        def _():
            xc_ref[...] = _from_slabs(xs_ref, tm).astype(BF16)
            acc_ref[...] = jnp.zeros_like(acc_ref)

        xc = xc_ref[...]
        gt = _dot(xc, wg_ref[0])
        up = _dot(xc, wu_ref[0])
        act = (gt * _sigmoid(gt) * up).astype(BF16)
        acc_ref[...] += _dot(act, wd_ref[0])

        @pl.when(k == pl.num_programs(1) - 1)
        def _():
            _to_slabs(os_ref, acc_ref[...])


def _combine_kernel(p1_ref, p2_ref, os_ref, info_ref, x_ref, g_ref, lnf_ref, o_ref,
                    buf_ref, sem, *, tm):
    def row_copy(t, p_ref, slot):
        src = pl.multiple_of(p_ref[t] * SLAB, SLAB)
        dst = pl.multiple_of(t * SLAB, SLAB)
        return pltpu.make_async_copy(os_ref.at[pl.ds(src, SLAB)],
                                     buf_ref.at[slot, pl.ds(dst, SLAB)], sem)

    def issue(t, carry):
        row_copy(t, p1_ref, 0).start()
        row_copy(t, p2_ref, 1).start()
        return carry

    lax.fori_loop(0, tm, issue, 0, unroll=8)

    def drain(t, carry):
        row_copy(t, p1_ref, 0).wait()
        row_copy(t, p2_ref, 1).wait()
        return carry

    lax.fori_loop(0, tm, drain, 0, unroll=8)

    info = info_ref[...]
    w1 = info[:, 2:3]
    w2 = info[:, 3:4]
    y = w1 * _from_slabs(buf_ref.at[0], tm) + w2 * _from_slabs(buf_ref.at[1], tm)
    xo = x_ref[...] + g_ref[0] * y
    ms = jnp.mean(xo * xo, axis=-1, keepdims=True)
    o_ref[...] = xo * lax.rsqrt(ms + EPS) * lnf_ref[...]


def _router(x, lnw, shift, scale, router_w, router_b, seq, tm):
    n, d = x.shape
    ne = router_w.shape[1]
    nt = n // tm
    tiles_per_seq = seq // tm
    rw = jnp.zeros((d, LANES), F32).at[:, :ne].set(router_w)
    rb = jnp.zeros((1, LANES), F32).at[0, :ne].set(router_b)
    bidx = lambda i: (i // tiles_per_seq, 0, 0)
    const = lambda i: (0, 0)
    return pl.pallas_call(
        functools.partial(_router_kernel, tm=tm),
        grid=(nt,),
        in_specs=[
            pl.BlockSpec((tm, d), lambda i: (i, 0)),
            pl.BlockSpec((1, d), const),
            pl.BlockSpec((1, 1, d), bidx),
            pl.BlockSpec((1, 1, d), bidx),
            pl.BlockSpec((d, LANES), const),
            pl.BlockSpec((1, LANES), const),
        ],
        out_specs=[
            pl.BlockSpec((tm * SLAB, LANES), lambda i: (i, 0)),
            pl.BlockSpec((tm, LANES), lambda i: (i, 0)),
            pl.BlockSpec((1, LANES), lambda i: (0, 0)),
        ],
        out_shape=[
            jax.ShapeDtypeStruct((n * SLAB, LANES), F32),
            jax.ShapeDtypeStruct((n, LANES), F32),
            jax.ShapeDtypeStruct((1, LANES), F32),
        ],
        scratch_shapes=[pltpu.VMEM((1, LANES), F32)],
        compiler_params=_cparams(("arbitrary",)),
        name="moe_router",
    )(x, lnw, shift, scale, rw, rb)


def _dispatch(pos1, pos2, hs, n_rows, tm):
    n = pos1.shape[0]
    smem = lambda: pl.BlockSpec((tm,), lambda i: (i,), memory_space=pltpu.SMEM)
    anywhere = lambda: pl.BlockSpec(memory_space=pl.ANY)
    return pl.pallas_call(
        functools.partial(_dispatch_kernel, tm=tm),
        grid=(n // tm,),
        in_specs=[smem(), smem(), anywhere(), anywhere()],
        out_specs=anywhere(),
        out_shape=jax.ShapeDtypeStruct((n_rows * SLAB, LANES), F32),
        scratch_shapes=[pltpu.SemaphoreType.DMA],
        input_output_aliases={3: 0},
        compiler_params=_cparams(("arbitrary",)),
        name="moe_dispatch",
    )(pos1, pos2, hs, jnp.zeros((n_rows * SLAB, LANES), F32))


def _experts(tile_expert, n_used, xs, wg, wu, wd, tm, tf):
    ne, d, f = wg.shape
    n_tiles = xs.shape[0] // (tm * SLAB)
    nk = f // tf
    rows = lambda j, k, te, nu: (jnp.minimum(j, nu[0] - 1), 0)
    chunk = lambda j, k, nu: jnp.where(j < nu[0], k, nk - 1)
    grid_spec = pltpu.PrefetchScalarGridSpec(
        num_scalar_prefetch=2,
        grid=(n_tiles, nk),
        in_specs=[
            pl.BlockSpec((tm * SLAB, LANES), rows),
            pl.BlockSpec((1, d, tf), lambda j, k, te, nu: (te[j], 0, chunk(j, k, nu))),
            pl.BlockSpec((1, d, tf), lambda j, k, te, nu: (te[j], 0, chunk(j, k, nu))),
            pl.BlockSpec((1, tf, d), lambda j, k, te, nu: (te[j], chunk(j, k, nu), 0)),
        ],
        out_specs=pl.BlockSpec((tm * SLAB, LANES), rows),
        scratch_shapes=[pltpu.VMEM((tm, d), BF16), pltpu.VMEM((tm, d), F32)],
    )
    return pl.pallas_call(
        functools.partial(_expert_kernel, tm=tm),
        grid_spec=grid_spec,
        out_shape=jax.ShapeDtypeStruct(xs.shape, F32),
        input_output_aliases={2: 0},
        compiler_params=_cparams(("arbitrary", "arbitrary")),
        name="moe_experts",
    )(tile_expert, n_used, xs, wg, wu, wd)


def _combine(pos1, pos2, os_, info, x, gate, ln_final, seq, tm):
    n, d = x.shape
    tiles_per_seq = seq // tm
    smem = lambda: pl.BlockSpec((tm,), lambda i: (i,), memory_space=pltpu.SMEM)
    return pl.pallas_call(
        functools.partial(_combine_kernel, tm=tm),
        grid=(n // tm,),
        in_specs=[
            smem(), smem(),
            pl.BlockSpec(memory_space=pl.ANY),
            pl.BlockSpec((tm, LANES), lambda i: (i, 0)),
            pl.BlockSpec((tm, d), lambda i: (i, 0)),
            pl.BlockSpec((1, 1, d), lambda i: (i // tiles_per_seq, 0, 0)),
            pl.BlockSpec((1, d), lambda i: (0, 0)),
        ],
        out_specs=pl.BlockSpec((tm, d), lambda i: (i, 0)),
        out_shape=jax.ShapeDtypeStruct((n, d), F32),
        scratch_shapes=[pltpu.VMEM((2, tm * SLAB, LANES), F32), pltpu.SemaphoreType.DMA],
        compiler_params=_cparams(("arbitrary",)),
        name="moe_combine",
    )(pos1, pos2, os_, info, x, gate, ln_final)


def _moe(x, lnw, shift, scale, gate, router_w, router_b, ln_final, wg, wu, wd, seq,
         tm=512, tf=1792):
    n, d = x.shape
    ne = router_w.shape[1]
    tm = min(tm, seq)
    hs, info, totals = _router(x, lnw, shift, scale, router_w, router_b, seq, tm)

    counts = totals[0, :ne].astype(jnp.int32)
    padded = (counts + tm - 1) // tm * tm
    ends = jnp.cumsum(padded)
    starts = ends - padded
    experts = jnp.arange(ne, dtype=jnp.int32)
    start_of = lambda e: jnp.sum(jnp.where(e[:, None] == experts, starts, 0), axis=1)
    e1, e2 = info[:, 0].astype(jnp.int32), info[:, 1].astype(jnp.int32)
    pos1 = start_of(e1) + info[:, 4].astype(jnp.int32)
    pos2 = start_of(e2) + info[:, 5].astype(jnp.int32)
    n_tiles = (2 * n) // tm + ne
    tile_expert = jnp.minimum(
        jnp.searchsorted(ends // tm, jnp.arange(n_tiles, dtype=jnp.int32), side="right"),
        ne - 1).astype(jnp.int32)
    n_used = (ends[-1:] // tm).astype(jnp.int32)

    xs = _dispatch(pos1, pos2, hs, n_tiles * tm, tm)
    os_ = _experts(tile_expert, n_used, xs, wg, wu, wd, tm, tf)
    return _combine(pos1, pos2, os_, info, x, gate, ln_final, seq, tm)


def kernel(x, c, rel_bias, ada_w, ada_b, ln_mix, ln_ffn, ln_final, even_w_in, even_w_out, lru_conv_w, lru_conv_b, lru_gate_a_w, lru_gate_a_b, lru_gate_x_w, lru_gate_x_b, lru_lambda, ffn_w_gate, ffn_w_up, ffn_w_down, odd_w_in, odd_w_out, diff_lambda_q1, diff_lambda_k1, diff_lambda_q2, diff_lambda_k2, diff_subln, router_w, router_b, moe_w_gate, moe_w_up, moe_w_down):
    batch, seq, d = x.shape
    n = batch * seq
    xf = x.reshape(n, d)
    mod = _adaln(c, ada_w, ada_b)
    part = lambda layer, j: mod[layer, :, j * d:(j + 1) * d].reshape(batch, 1, d)
    row = lambda v: v.reshape(1, -1)

    sh1, sc1, g1, sh2, sc2, g2 = (part(0, j) for j in range(6))
    qkv_w = 3 * SB_WIDTH
    qkv, xg = _inproj(xf, row(ln_mix[0]), sh1, sc1, even_w_in[0].astype(BF16),
                      ((0, qkv_w), (qkv_w, qkv_w + 2 * LRU_WIDTH)), (BF16, F32), seq)
    ya = _sb_attention(qkv, batch, seq)
    yb = _rg_lru(xg, lru_conv_w[0], lru_conv_b[0], lru_gate_a_w[0], lru_gate_a_b[0],
                 lru_gate_x_w[0], lru_gate_x_b[0], lru_lambda[0], batch, seq)
    xf = _outproj(xf, g1, even_w_out[0].astype(BF16), (ya, yb), seq)
    xf = _ffn(xf, row(ln_ffn[0]), sh2, sc2, g2, ffn_w_gate[0].astype(BF16),
              ffn_w_up[0].astype(BF16), ffn_w_down[0].astype(BF16), seq)

    sh1, sc1, g1, sh2, sc2, g2 = (part(1, j) for j in range(6))
    lambda_init = 0.8 - 0.6 * math.exp(-0.3 * 1)
    (qkv,) = _inproj(xf, row(ln_mix[1]), sh1, sc1, odd_w_in[0].astype(BF16),
                     ((0, 2 * DIFF_QK + DIFF_V),), (BF16,), seq)
    yo = _diff_attention(qkv, rel_bias, diff_lambda_q1[0], diff_lambda_k1[0], diff_lambda_q2[0],
                         diff_lambda_k2[0], diff_subln[0], lambda_init, batch, seq)
    xf = _outproj(xf, g1, odd_w_out[0].astype(BF16), (yo,), seq)
    out = _moe(xf, row(ln_ffn[1]), sh2, sc2, g2, router_w[0], router_b[0], row(ln_final),
               moe_w_gate[0].astype(BF16), moe_w_up[0].astype(BF16), moe_w_down[0].astype(BF16), seq)
    return out.reshape(batch, seq, d)
```

```python
import functools
import math

import jax
import jax.numpy as jnp
from jax import lax
from jax.experimental import pallas as pl
from jax.experimental.pallas import tpu as pltpu

F32 = jnp.float32
BF16 = jnp.bfloat16

D_MODEL = 1024
HEAD_DIM = 64
LANES = 128
SLAB = D_MODEL // LANES
SB_HEADS = 8
SB_WIDTH = SB_HEADS * HEAD_DIM
LRU_WIDTH = 512
LRU_BLOCKS = 8
CONV_WIDTH = 4
LRU_C = 8.0
DIFF_HEADS = 8
DIFF_QK = 2 * DIFF_HEADS * HEAD_DIM
DIFF_V = DIFF_HEADS * 2 * HEAD_DIM
N_BUCKETS = 32
MAX_EXACT = N_BUCKETS // 2
MAX_DISTANCE = 128
D_FF_DENSE = 2816
N_EXPERTS = 8
D_FF_EXPERT = 3584
EPS = 1e-6
NEG_BIG = -1e30
UNDERFLOW_LOG = 110.0

VMEM_LIMIT = 56 * 1024 * 1024


def _cparams(sem):
    return pltpu.CompilerParams(dimension_semantics=sem, vmem_limit_bytes=VMEM_LIMIT)


def _dot(a, b):
    return jnp.dot(a, b, preferred_element_type=F32)


def _dot_nt(a, b):
    return lax.dot_general(a, b, (((1,), (1,)), ((), ())), preferred_element_type=F32)


def _split_bf16(x):
    hi = x.astype(BF16)
    lo = (x - hi.astype(F32)).astype(BF16)
    return hi, lo


def _sigmoid(x):
    return 1.0 / (1.0 + jnp.exp(-x))


def _norm_mod(x, lnw, shift, scale):
    ms = jnp.mean(x * x, axis=-1, keepdims=True)
    n = x * lax.rsqrt(ms + EPS)
    return (n * lnw) * (1.0 + scale) + shift


def _adaln_kernel(c_ref, w_ref, b_ref, o_ref):
    c = c_ref[...]
    cond = c * _sigmoid(c)
    ch, cl = _split_bf16(cond)
    wh, wl = _split_bf16(w_ref[0])
    o_ref[0] = _dot(ch, wh) + _dot(ch, wl) + _dot(cl, wh) + b_ref[0]


def _adaln(c, ada_w, ada_b):
    depth, d, n6 = ada_w.shape
    b = c.shape[0]
    tn = 1536
    return pl.pallas_call(
        _adaln_kernel,
        grid=(depth, n6 // tn),
        in_specs=[
            pl.BlockSpec((b, d), lambda l, j: (0, 0)),
            pl.BlockSpec((1, d, tn), lambda l, j: (l, 0, j)),
            pl.BlockSpec((1, 1, tn), lambda l, j: (l, 0, j)),
        ],
        out_specs=pl.BlockSpec((1, b, tn), lambda l, j: (l, 0, j)),
        out_shape=jax.ShapeDtypeStruct((depth, b, n6), F32),
        compiler_params=_cparams(("arbitrary", "arbitrary")),
        name="adaln",
    )(c, ada_w, ada_b.reshape(depth, 1, n6))


def _inproj_kernel(x_ref, lnw_ref, sh_ref, sc_ref, w_ref, *o_refs, splits):
    h = _norm_mod(x_ref[...], lnw_ref[...], sh_ref[0], sc_ref[0]).astype(BF16)
    for o_ref, (lo, hi) in zip(o_refs, splits):
        o_ref[...] = _dot(h, w_ref[:, lo:hi]).astype(o_ref.dtype)


def _inproj(x, lnw, shift, scale, w, splits, dtypes, seq, tm=512):
    n, d = x.shape
    tiles_per_seq = seq // tm
    bidx = lambda i: (i // tiles_per_seq, 0, 0)
    return pl.pallas_call(
        functools.partial(_inproj_kernel, splits=splits),
        grid=(n // tm,),
        in_specs=[
            pl.BlockSpec((tm, d), lambda i: (i, 0)),
            pl.BlockSpec((1, d), lambda i: (0, 0)),
            pl.BlockSpec((1, 1, d), bidx),
            pl.BlockSpec((1, 1, d), bidx),
            pl.BlockSpec(w.shape, lambda i: (0, 0)),
        ],
        out_specs=[pl.BlockSpec((tm, hi - lo), lambda i: (i, 0)) for lo, hi in splits],
        out_shape=[jax.ShapeDtypeStruct((n, hi - lo), dt) for (lo, hi), dt in zip(splits, dtypes)],
        compiler_params=_cparams(("arbitrary",)),
        name="inproj",
    )(x, lnw, shift, scale, w)


def _sb_kernel(q_ref, k_ref, v_ref, o_ref, acc_ref, c_ref, *, tq, tk, hp):
    qi = pl.program_id(2)
    ratio = tq // tk
    n_tiles = (qi + 1) * ratio
    nc = tk // LANES
    nh = 2 * hp
    lane = lax.broadcasted_iota(jnp.int32, (tq, LANES), 1)
    blk = lambda p: slice(p * LANES, (p + 1) * LANES)
    qs = []
    for p in range(hp):
        q = q_ref[:, blk(p)] * (HEAD_DIM ** -0.5)
        zero = jnp.zeros_like(q)
        qs += [jnp.where(lane < HEAD_DIM, q, zero), jnp.where(lane >= HEAD_DIM, q, zero)]
    rj = lax.broadcasted_iota(jnp.int32, (tk, tk), 0)
    cs = lax.broadcasted_iota(jnp.int32, (tk, tk), 1)
    neg_upper = jnp.where(rj > cs, -1.0, 0.0).astype(BF16)

    acc_ref[...] = jnp.zeros_like(acc_ref)
    c_ref[...] = jnp.zeros_like(c_ref)

    def causal_mask(j):
        q_pos = qi * tq + lax.broadcasted_iota(jnp.int32, (tq, tk), 0)
        k_pos = j * tk + lax.broadcasted_iota(jnp.int32, (tq, tk), 1)
        return k_pos < q_pos

    def tile(j, masked):
        start = pl.multiple_of(j * tk, tk)
        for hh in range(nh):
            ks = k_ref[pl.ds(start, tk), blk(hh // 2)]
            vs = v_ref[pl.ds(start, tk), blk(hh // 2)]
            z = _dot_nt(qs[hh], ks)
            sp = jnp.maximum(z, 0.0) + jnp.log(1.0 + jnp.exp(-jnp.abs(z)))
            if masked:
                sp = jnp.where(causal_mask(j), sp, 0.0)
            hi, lo = _split_bf16(sp)
            later = _dot(hi, neg_upper) + _dot(lo, neg_upper)
            c = c_ref[hh]
            w = jnp.concatenate(
                [jnp.exp((z - sp)[:, i * LANES:(i + 1) * LANES]
                         + later[:, i * LANES:(i + 1) * LANES] - c) for i in range(nc)], axis=1)
            if masked:
                w = jnp.where(causal_mask(j), w, 0.0)
            acc_ref[hh] += _dot(w.astype(BF16), vs)
            c_ref[hh] = c + jnp.broadcast_to(jnp.sum(sp, axis=1, keepdims=True), (tq, LANES))

    for d in range(ratio):
        tile(n_tiles - 1 - d, True)

    def cond(state):
        n, c_min = state
        return (n < n_tiles - ratio) & (c_min < UNDERFLOW_LOG)

    def body(state):
        n, _ = state
        tile(n_tiles - ratio - 1 - n, False)
        return n + 1, jnp.min(c_ref[...])

    lax.while_loop(cond, body, (jnp.int32(0), jnp.min(c_ref[...])))

    for p in range(hp):
        o_ref[:, blk(p)] = jnp.where(lane < HEAD_DIM, acc_ref[2 * p],
                                     acc_ref[2 * p + 1]).astype(o_ref.dtype)


def _sb_attention(qkv, batch, seq, tq=512, tk=256, hp=1):
    tq = min(tq, seq)
    tk = min(tk, tq)
    n = qkv.shape[0]
    width = hp * LANES
    groups = SB_WIDTH // width
    nq = seq // tq
    return pl.pallas_call(
        functools.partial(_sb_kernel, tq=tq, tk=tk, hp=hp),
        grid=(batch, groups, nq),
        in_specs=[
            pl.BlockSpec((tq, width), lambda b, p, i: (b * nq + i, p)),
            pl.BlockSpec((seq, width), lambda b, p, i: (b, groups + p)),
            pl.BlockSpec((seq, width), lambda b, p, i: (b, 2 * groups + p)),
        ],
        out_specs=pl.BlockSpec((tq, width), lambda b, p, i: (b * nq + i, p)),
        out_shape=jax.ShapeDtypeStruct((n, SB_WIDTH), BF16),
        scratch_shapes=[
            pltpu.VMEM((2 * hp, tq, LANES), F32),
            pltpu.VMEM((2 * hp, tq, LANES), F32),
        ],
        compiler_params=_cparams(("arbitrary", "arbitrary", "arbitrary")),
        name="sb_attention",
    )(qkv, qkv, qkv)


def _lru_kernel(xg_ref, cw_ref, cb_ref, wg_ref, bg_ref, lam_ref, o_ref,
                ext_ref, a_ref, u_ref, h_ref, carry_ref, *, tt):
    c = LRU_WIDTH
    halo = 8

    @pl.when(pl.program_id(1) == 0)
    def _():
        ext_ref[0:halo, :] = jnp.zeros((halo, c), F32)
        carry_ref[...] = jnp.zeros_like(carry_ref)

    xb = xg_ref[:, 0:c]
    gb = xg_ref[:, c:2 * c]
    ext_ref[halo:halo + tt, :] = xb
    xc = cb_ref[...] + cw_ref[CONV_WIDTH - 1:CONV_WIDTH, :] * xb
    for i in range(CONV_WIDTH - 1):
        off = halo - (CONV_WIDTH - 1) + i
        xc = xc + cw_ref[i:i + 1, :] * ext_ref[off:off + tt, :]
    ext_ref[0:halo, :] = xb[tt - halo:tt, :]

    gates = _dot(xc.astype(BF16), wg_ref[...]) + bg_ref[...]
    r = _sigmoid(gates[:, 0:c])
    ig = _sigmoid(gates[:, c:2 * c])
    lam = lam_ref[...]
    log_sig_lam = jnp.minimum(lam, 0.0) - jnp.log(1.0 + jnp.exp(-jnp.abs(lam)))
    a = jnp.exp(LRU_C * r * log_sig_lam)
    u = jnp.sqrt(1.0 - a * a) * (ig * xc)

    r8 = lax.broadcasted_iota(jnp.int32, (tt, c), 0) & 7
    for s in (1, 2, 4):
        a_prev = pltpu.roll(a, s, axis=0)
        u_prev = pltpu.roll(u, s, axis=0)
        valid = r8 >= s
        u = jnp.where(valid, a * u_prev + u, u)
        a = jnp.where(valid, a * a_prev, a)
    a_ref[...] = a
    u_ref[...] = u

    def body(g, carry):
        off = pl.multiple_of(g * 8, 8)
        h = a_ref[pl.ds(off, 8), :] * carry + u_ref[pl.ds(off, 8), :]
        h_ref[pl.ds(off, 8), :] = h
        return jnp.broadcast_to(h[7:8, :], (8, c))

    carry_ref[...] = lax.fori_loop(0, tt // 8, body, carry_ref[...])

    gelu = 0.5 * gb * (1.0 + jnp.tanh(math.sqrt(2.0 / math.pi) * (gb + 0.044715 * (gb * gb * gb))))
    o_ref[...] = (h_ref[...] * gelu).astype(o_ref.dtype)


def _block_diag(w):
    g, bi, bo = w.shape
    eye = jnp.eye(g, dtype=w.dtype)
    return (w[:, :, None, :] * eye[:, None, :, None]).reshape(g * bi, g * bo)


def _rg_lru(xg, conv_w, conv_b, ga_w, ga_b, gx_w, gx_b, lam, batch, seq, tt=512):
    tt = min(tt, seq)
    n = xg.shape[0]
    c = LRU_WIDTH
    nt = seq // tt
    wg = jnp.concatenate([_block_diag(ga_w), _block_diag(gx_w)], axis=1).astype(BF16)
    bg = jnp.concatenate([ga_b, gx_b]).reshape(1, 2 * c)
    const = lambda b, t: (0, 0)
    return pl.pallas_call(
        functools.partial(_lru_kernel, tt=tt),
        grid=(batch, nt),
        in_specs=[
            pl.BlockSpec((tt, 2 * c), lambda b, t: (b * nt + t, 0)),
            pl.BlockSpec((CONV_WIDTH, c), const),
            pl.BlockSpec((1, c), const),
            pl.BlockSpec((c, 2 * c), const),
            pl.BlockSpec((1, 2 * c), const),
            pl.BlockSpec((1, c), const),
        ],
        out_specs=pl.BlockSpec((tt, c), lambda b, t: (b * nt + t, 0)),
        out_shape=jax.ShapeDtypeStruct((n, c), BF16),
        scratch_shapes=[
            pltpu.VMEM((tt + 8, c), F32),
            pltpu.VMEM((tt, c), F32),
            pltpu.VMEM((tt, c), F32),
            pltpu.VMEM((tt, c), F32),
            pltpu.VMEM((8, c), F32),
        ],
        compiler_params=_cparams(("arbitrary", "arbitrary")),
        name="rg_lru",
    )(xg, conv_w, conv_b.reshape(1, c), wg, bg, lam.reshape(1, c))


def _outproj_kernel(*refs, widths):
    x_ref, g_ref, w_ref = refs[0], refs[1], refs[2]
    y_refs = refs[3:3 + len(widths)]
    o_ref = refs[3 + len(widths)]
    mix = None
    off = 0
    for y_ref, wd in zip(y_refs, widths):
        part = _dot(y_ref[...], w_ref[off:off + wd, :])
        mix = part if mix is None else mix + part
        off += wd
    o_ref[...] = x_ref[...] + g_ref[0] * mix


def _outproj(x, gate, w, ys, seq, tm=512):
    n, d = x.shape
    tiles_per_seq = seq // tm
    widths = tuple(y.shape[1] for y in ys)
    return pl.pallas_call(
        functools.partial(_outproj_kernel, widths=widths),
        grid=(n // tm,),
        in_specs=[
            pl.BlockSpec((tm, d), lambda i: (i, 0)),
            pl.BlockSpec((1, 1, d), lambda i: (i // tiles_per_seq, 0, 0)),
            pl.BlockSpec(w.shape, lambda i: (0, 0)),
        ] + [pl.BlockSpec((tm, wd), lambda i: (i, 0)) for wd in widths],
        out_specs=pl.BlockSpec((tm, d), lambda i: (i, 0)),
        out_shape=jax.ShapeDtypeStruct((n, d), F32),
        compiler_params=_cparams(("arbitrary",)),
        name="outproj",
    )(x, gate, w, *ys)


def _ffn_kernel(x_ref, lnw_ref, sh_ref, sc_ref, g_ref, wg_ref, wu_ref, wd_ref, o_ref,
                h_ref, acc_ref):
    k = pl.program_id(1)

    @pl.when(k == 0)
    def _():
        h_ref[...] = _norm_mod(x_ref[...], lnw_ref[...], sh_ref[0], sc_ref[0]).astype(BF16)
        acc_ref[...] = jnp.zeros_like(acc_ref)

    h = h_ref[...]
    gt = _dot(h, wg_ref[...])
    up = _dot(h, wu_ref[...])
    act = (gt * _sigmoid(gt) * up).astype(BF16)
    acc_ref[...] += _dot(act, wd_ref[...])

    @pl.when(k == pl.num_programs(1) - 1)
    def _():
        o_ref[...] = x_ref[...] + g_ref[0] * acc_ref[...]


def _ffn(x, lnw, shift, scale, gate, wg, wu, wd, seq, tm=512, tf=1408):
    n, d = x.shape
    f = wg.shape[1]
    tiles_per_seq = seq // tm
    bidx = lambda i, k: (i // tiles_per_seq, 0, 0)
    return pl.pallas_call(
        _ffn_kernel,
        grid=(n // tm, f // tf),
        in_specs=[
            pl.BlockSpec((tm, d), lambda i, k: (i, 0)),
            pl.BlockSpec((1, d), lambda i, k: (0, 0)),
            pl.BlockSpec((1, 1, d), bidx),
            pl.BlockSpec((1, 1, d), bidx),
            pl.BlockSpec((1, 1, d), bidx),
            pl.BlockSpec((d, tf), lambda i, k: (0, k)),
            pl.BlockSpec((d, tf), lambda i, k: (0, k)),
            pl.BlockSpec((tf, d), lambda i, k: (k, 0)),
        ],
        out_specs=pl.BlockSpec((tm, d), lambda i, k: (i, 0)),
        out_shape=jax.ShapeDtypeStruct((n, d), F32),
        scratch_shapes=[pltpu.VMEM((tm, d), BF16), pltpu.VMEM((tm, d), F32)],
        compiler_params=_cparams(("arbitrary", "arbitrary")),
        name="ffn_dense",
    )(x, lnw, shift, scale, gate, wg, wu, wd)


def _bias_kernel(rb_ref, o_ref, *, tq, tk, n_near):
    h = pl.program_id(0)
    t = pl.program_id(1)
    row = lax.broadcasted_iota(jnp.int32, (tq, tk), 0)
    col = lax.broadcasted_iota(jnp.int32, (tq, tk), 1)
    rel = row - col + (n_near - 1 - t) * tk - (tq - tk)
    nn = jnp.maximum(rel, 0)
    nf = jnp.maximum(nn, 1).astype(F32)
    large = MAX_EXACT + (jnp.log(nf / MAX_EXACT) / math.log(MAX_DISTANCE / MAX_EXACT)
                         * (N_BUCKETS - MAX_EXACT)).astype(jnp.int32)
    large = jnp.minimum(large, N_BUCKETS - 1)
    bucket = jnp.where(nn < MAX_EXACT, nn, large)
    bias = jnp.zeros((tq, tk), F32)
    for b in range(N_BUCKETS):
        bias = jnp.where(bucket == b, rb_ref[b, h], bias)
    o_ref[0, 0] = jnp.where(rel >= 0, bias, NEG_BIG)


def _near_bias(rel_bias, tq, tk, n_near):
    return pl.pallas_call(
        functools.partial(_bias_kernel, tq=tq, tk=tk, n_near=n_near),
        grid=(DIFF_HEADS, n_near),
        in_specs=[pl.BlockSpec(memory_space=pltpu.SMEM)],
        out_specs=pl.BlockSpec((1, 1, tq, tk), lambda h, t: (h, t, 0, 0)),
        out_shape=jax.ShapeDtypeStruct((DIFF_HEADS, n_near, tq, tk), F32),
        compiler_params=_cparams(("arbitrary", "arbitrary")),
        name="t5_bias_tiles",
    )(rel_bias)


def _diff_kernel(rb_ref, q_ref, k_ref, v_ref, nb_ref, lq1_ref, lk1_ref, lq2_ref, lk2_ref,
                 sub_ref, o_ref, m_ref, acc_ref, vext_ref, z_ref, *, tq, tk, n_near,
                 lambda_init):
    h = pl.program_id(1)
    qi = pl.program_id(2)
    ratio = tq // tk
    lane = lax.broadcasted_iota(jnp.int32, (tq, LANES), 1)
    q = q_ref[...] * (HEAD_DIM ** -0.5)
    zero = jnp.zeros_like(q)
    qs = (jnp.where(lane < HEAD_DIM, q, zero), jnp.where(lane >= HEAD_DIM, q, zero))

    @pl.when(qi == 0)
    def _():
        vext_ref[:, 0:LANES] = v_ref[...]
        vext_ref[:, LANES:2 * LANES] = jnp.ones(v_ref.shape, BF16)

    m_ref[...] = jnp.full_like(m_ref, NEG_BIG)
    acc_ref[...] = jnp.zeros_like(acc_ref)
    nc = tk // LANES

    def scores(j, slot):
        start = pl.multiple_of(j * tk, tk)
        ks = k_ref[pl.ds(start, tk), :]
        for mm in range(2):
            z_ref[slot, mm] = _dot_nt(qs[mm], ks)

    def softmax_pv(j, slot, bias, bias_is_scalar):
        start = pl.multiple_of(j * tk, tk)
        vs = vext_ref[pl.ds(start, tk), :]
        for mm in range(2):
            zc = [z_ref[slot, mm, :, c * LANES:(c + 1) * LANES] for c in range(nc)]
            if not bias_is_scalar:
                zc = [zc[c] + bias[:, c * LANES:(c + 1) * LANES] for c in range(nc)]
            part = functools.reduce(jnp.maximum, zc)
            rm = jnp.broadcast_to(jnp.max(part, axis=1, keepdims=True), (tq, LANES))
            m_old = m_ref[mm]
            if bias_is_scalar:
                m_new = jnp.maximum(m_old, rm + bias)
                shift = m_new - bias
            else:
                m_new = jnp.maximum(m_old, rm)
                shift = m_new
            alpha = jnp.exp(m_old - m_new)
            p = jnp.concatenate([jnp.exp(z - shift) for z in zc], axis=1).astype(BF16)
            pv = _dot(p, vs)
            for c in range(2):
                sl = slice(c * LANES, (c + 1) * LANES)
                acc_ref[mm, :, sl] = alpha * acc_ref[mm, :, sl] + pv[:, sl]
            m_ref[mm] = m_new

    n_tiles = (qi + 1) * ratio
    n_far = jnp.maximum(n_tiles - n_near, 0)
    far_bias = rb_ref[N_BUCKETS - 1, h]
    scores(0, 0)

    def body(j, carry):
        slot = j & 1
        softmax_pv(j, slot, far_bias, True)
        scores(j + 1, 1 - slot)
        return carry

    lax.fori_loop(0, n_far, body, 0)

    for t in range(n_near):
        j = n_tiles - n_near + t

        @pl.when(j >= 0)
        def _():
            slot = j & 1
            softmax_pv(j, slot, nb_ref[0, t], False)
            if t < n_near - 1:
                scores(j + 1, 1 - slot)

    lam = (jnp.exp(jnp.sum(lq1_ref[...] * lk1_ref[...], axis=1, keepdims=True))
           - jnp.exp(jnp.sum(lq2_ref[...] * lk2_ref[...], axis=1, keepdims=True))
           + lambda_init)
    o = (acc_ref[0, :, 0:LANES] / acc_ref[0, :, LANES:2 * LANES]
         - lam * (acc_ref[1, :, 0:LANES] / acc_ref[1, :, LANES:2 * LANES]))
    ms = jnp.mean(o * o, axis=-1, keepdims=True)
    o = o * lax.rsqrt(ms + EPS) * sub_ref[...] * (1.0 - lambda_init)
    o_ref[...] = o.astype(o_ref.dtype)


def _diff_attention(qkv, rel_bias, lq1, lk1, lq2, lk2, subln, lambda_init, batch, seq,
                    tq=512, tk=512):
    tq = min(tq, seq)
    tk = min(tk, tq)
    assert tk >= MAX_DISTANCE
    n = qkv.shape[0]
    nq = seq // tq
    nh = DIFF_HEADS
    n_near = tq // tk + 1
    near = _near_bias(rel_bias, tq, tk, n_near)
    vec = lambda a: a.reshape(1, -1)
    cvec = pl.BlockSpec((1, HEAD_DIM), lambda b, h, i: (0, 0))
    return pl.pallas_call(
        functools.partial(_diff_kernel, tq=tq, tk=tk, n_near=n_near, lambda_init=lambda_init),
        grid=(batch, nh, nq),
        in_specs=[
            pl.BlockSpec(memory_space=pltpu.SMEM),
            pl.BlockSpec((tq, LANES), lambda b, h, i: (b * nq + i, h)),
            pl.BlockSpec((seq, LANES), lambda b, h, i: (b, nh + h)),
            pl.BlockSpec((seq, LANES), lambda b, h, i: (b, 2 * nh + h)),
            pl.BlockSpec((1, n_near, tq, tk), lambda b, h, i: (h, 0, 0, 0)),
            cvec, cvec, cvec, cvec,
            pl.BlockSpec((1, LANES), lambda b, h, i: (0, 0)),
        ],
        out_specs=pl.BlockSpec((tq, LANES), lambda b, h, i: (b * nq + i, h)),
        out_shape=jax.ShapeDtypeStruct((n, DIFF_V), BF16),
        scratch_shapes=[
            pltpu.VMEM((2, tq, LANES), F32),
            pltpu.VMEM((2, tq, 2 * LANES), F32),
            pltpu.VMEM((seq, 2 * LANES), BF16),
            pltpu.VMEM((2, 2, tq, tk), F32),
        ],
        compiler_params=_cparams(("arbitrary", "arbitrary", "arbitrary")),
        name="diff_attention",
    )(rel_bias, qkv, qkv, qkv, near, vec(lq1), vec(lk1), vec(lq2), vec(lk2), vec(subln))


def _to_slabs(ref, value):
    rows = value.shape[0]
    for j in range(SLAB):
        ref[pl.ds(j, rows, stride=SLAB), :] = value[:, j * LANES:(j + 1) * LANES]


def _from_slabs(ref, rows):
    return jnp.concatenate([ref[pl.ds(j, rows, stride=SLAB), :] for j in range(SLAB)], axis=1)


def _router_kernel(x_ref, lnw_ref, sh_ref, sc_ref, rw_ref, rb_ref,
                   hs_ref, info_ref, totals_ref, run_ref, *, tm):
    @pl.when(pl.program_id(0) == 0)
    def _():
        run_ref[...] = jnp.zeros_like(run_ref)

    lane = lax.broadcasted_iota(jnp.int32, (tm, LANES), 1)
    h = _norm_mod(x_ref[...], lnw_ref[...], sh_ref[0], sc_ref[0])
    _to_slabs(hs_ref, h)
    hh, hl = _split_bf16(h)
    wh, wl = _split_bf16(rw_ref[...])
    logits = _dot(hh, wh) + _dot(hl, wh) + _dot(hh, wl) + rb_ref[...]
    neg_inf = jnp.float32(-jnp.inf)
    lanef = lane.astype(F32)
    logits = jnp.where(lane < N_EXPERTS, logits, neg_inf)
    m1 = jnp.max(logits, axis=1, keepdims=True)
    i1 = jnp.min(jnp.where(logits == m1, lanef, float(LANES)), axis=1, keepdims=True)
    rest = jnp.where(lanef == i1, neg_inf, logits)
    m2 = jnp.max(rest, axis=1, keepdims=True)
    i2 = jnp.min(jnp.where(rest == m2, lanef, float(LANES)), axis=1, keepdims=True)
    w1 = 1.0 / (1.0 + jnp.exp(m2 - m1))
    w2 = 1.0 - w1
    picked_f = jnp.where((lanef == i1) | (lanef == i2), 1.0, 0.0)
    ri = lax.broadcasted_iota(jnp.int32, (tm, tm), 0)
    ci = lax.broadcasted_iota(jnp.int32, (tm, tm), 1)
    earlier = jnp.where(ci < ri, 1.0, 0.0).astype(BF16)
    rank = _dot(earlier, picked_f.astype(BF16)) + run_ref[...]
    pick = lambda idx: jnp.sum(jnp.where(lanef == idx, rank, 0.0), axis=1, keepdims=True)
    r1, r2 = pick(i1), pick(i2)
    cols = (i1, i2, w1, w2, r1, r2)
    info = jnp.zeros((tm, LANES), F32)
    for c, v in enumerate(cols):
        info = jnp.where(lane == c, v, info)
    info_ref[...] = info
    run_ref[...] += jnp.sum(picked_f, axis=0, keepdims=True)
    totals_ref[...] = run_ref[...]


def _dispatch_kernel(p1_ref, p2_ref, hs_ref, zero_ref, xs_ref, sem, *, tm):
    del zero_ref

    def row_copy(t, p_ref):
        src = pl.multiple_of(t * SLAB, SLAB)
        dst = pl.multiple_of(p_ref[t] * SLAB, SLAB)
        return pltpu.make_async_copy(hs_ref.at[pl.ds(src, SLAB)], xs_ref.at[pl.ds(dst, SLAB)], sem)

    def issue(t, carry):
        row_copy(t, p1_ref).start(priority=0)
        row_copy(t, p2_ref).start(priority=1)
        return carry

    lax.fori_loop(0, tm, issue, 0, unroll=8)

    def drain(t, carry):
        row_copy(t, p1_ref).wait()
        row_copy(t, p2_ref).wait()
        return carry

    lax.fori_loop(0, tm, drain, 0, unroll=8)


def _expert_kernel(te_ref, nu_ref, xs_ref, wg_ref, wu_ref, wd_ref, os_ref, xc_ref, acc_ref, *, tm):
    j = pl.program_id(0)
    k = pl.program_id(1)

    @pl.when(j < nu_ref[0])
    def _():
        @pl.when(k == 0)
        def _():
            xc_ref[...] = _from_slabs(xs_ref, tm).astype(BF16)
            acc_ref[...] = jnp.zeros_like(acc_ref)

        xc = xc_ref[...]
        gt = _dot(xc, wg_ref[0])
        up = _dot(xc, wu_ref[0])
        act = (gt * _sigmoid(gt) * up).astype(BF16)
        acc_ref[...] += _dot(act, wd_ref[0])

        @pl.when(k == pl.num_programs(1) - 1)
        def _():
            _to_slabs(os_ref, acc_ref[...])


def _combine_kernel(p1_ref, p2_ref, os_ref, info_ref, x_ref, g_ref, lnf_ref, o_ref,
                    buf_ref, sem, *, tm):
    def row_copy(t, p_ref, slot):
        src = pl.multiple_of(p_ref[t] * SLAB, SLAB)
        dst = pl.multiple_of(t * SLAB, SLAB)
        return pltpu.make_async_copy(os_ref.at[pl.ds(src, SLAB)],
                                     buf_ref.at[slot, pl.ds(dst, SLAB)], sem)

    def issue(t, carry):
        row_copy(t, p1_ref, 0).start(priority=0)
        row_copy(t, p2_ref, 1).start(priority=1)
        return carry

    lax.fori_loop(0, tm, issue, 0, unroll=8)

    def drain(t, carry):
        row_copy(t, p1_ref, 0).wait()
        row_copy(t, p2_ref, 1).wait()
        return carry

    lax.fori_loop(0, tm, drain, 0, unroll=8)

    info = info_ref[...]
    w1 = info[:, 2:3]
    w2 = info[:, 3:4]
    y = w1 * _from_slabs(buf_ref.at[0], tm) + w2 * _from_slabs(buf_ref.at[1], tm)
    xo = x_ref[...] + g_ref[0] * y
    ms = jnp.mean(xo * xo, axis=-1, keepdims=True)
    o_ref[...] = xo * lax.rsqrt(ms + EPS) * lnf_ref[...]


def _router(x, lnw, shift, scale, router_w, router_b, seq, tm):
    n, d = x.shape
    ne = router_w.shape[1]
    nt = n // tm
    tiles_per_seq = seq // tm
    rw = jnp.zeros((d, LANES), F32).at[:, :ne].set(router_w)
    rb = jnp.zeros((1, LANES), F32).at[0, :ne].set(router_b)
    bidx = lambda i: (i // tiles_per_seq, 0, 0)
    const = lambda i: (0, 0)
    return pl.pallas_call(
        functools.partial(_router_kernel, tm=tm),
        grid=(nt,),
        in_specs=[
            pl.BlockSpec((tm, d), lambda i: (i, 0)),
            pl.BlockSpec((1, d), const),
            pl.BlockSpec((1, 1, d), bidx),
            pl.BlockSpec((1, 1, d), bidx),
            pl.BlockSpec((d, LANES), const),
            pl.BlockSpec((1, LANES), const),
        ],
        out_specs=[
            pl.BlockSpec((tm * SLAB, LANES), lambda i: (i, 0)),
            pl.BlockSpec((tm, LANES), lambda i: (i, 0)),
            pl.BlockSpec((1, LANES), lambda i: (0, 0)),
        ],
        out_shape=[
            jax.ShapeDtypeStruct((n * SLAB, LANES), F32),
            jax.ShapeDtypeStruct((n, LANES), F32),
            jax.ShapeDtypeStruct((1, LANES), F32),
        ],
        scratch_shapes=[pltpu.VMEM((1, LANES), F32)],
        compiler_params=_cparams(("arbitrary",)),
        name="moe_router",
    )(x, lnw, shift, scale, rw, rb)


def _dispatch(pos1, pos2, hs, n_rows, tm):
    n = pos1.shape[0]
    smem = lambda: pl.BlockSpec((tm,), lambda i: (i,), memory_space=pltpu.SMEM)
    anywhere = lambda: pl.BlockSpec(memory_space=pl.ANY)
    return pl.pallas_call(
        functools.partial(_dispatch_kernel, tm=tm),
        grid=(n // tm,),
        in_specs=[smem(), smem(), pl.BlockSpec((tm * SLAB, LANES), lambda i: (i, 0)), anywhere()],
        out_specs=anywhere(),
        out_shape=jax.ShapeDtypeStruct((n_rows * SLAB, LANES), F32),
        scratch_shapes=[pltpu.SemaphoreType.DMA],
        input_output_aliases={3: 0},
        compiler_params=_cparams(("arbitrary",)),
        name="moe_dispatch",
    )(pos1, pos2, hs, jnp.zeros((n_rows * SLAB, LANES), F32))


def _experts(tile_expert, n_used, xs, wg, wu, wd, tm, tf):
    ne, d, f = wg.shape
    n_tiles = xs.shape[0] // (tm * SLAB)
    nk = f // tf
    rows = lambda j, k, te, nu: (jnp.minimum(j, nu[0] - 1), 0)
    chunk = lambda j, k, nu: jnp.where(j < nu[0], k, nk - 1)
    grid_spec = pltpu.PrefetchScalarGridSpec(
        num_scalar_prefetch=2,
        grid=(n_tiles, nk),
        in_specs=[
            pl.BlockSpec((tm * SLAB, LANES), rows),
            pl.BlockSpec((1, d, tf), lambda j, k, te, nu: (te[j], 0, chunk(j, k, nu))),
            pl.BlockSpec((1, d, tf), lambda j, k, te, nu: (te[j], 0, chunk(j, k, nu))),
            pl.BlockSpec((1, tf, d), lambda j, k, te, nu: (te[j], chunk(j, k, nu), 0)),
        ],
        out_specs=pl.BlockSpec((tm * SLAB, LANES), rows),
        scratch_shapes=[pltpu.VMEM((tm, d), BF16), pltpu.VMEM((tm, d), F32)],
    )
    return pl.pallas_call(
        functools.partial(_expert_kernel, tm=tm),
        grid_spec=grid_spec,
        out_shape=jax.ShapeDtypeStruct(xs.shape, F32),
        input_output_aliases={2: 0},
        compiler_params=_cparams(("arbitrary", "arbitrary")),
        name="moe_experts",
    )(tile_expert, n_used, xs, wg, wu, wd)


def _combine(pos1, pos2, os_, info, x, gate, ln_final, seq, tm):
    n, d = x.shape
    tiles_per_seq = seq // tm
    smem = lambda: pl.BlockSpec((tm,), lambda i: (i,), memory_space=pltpu.SMEM)
    return pl.pallas_call(
        functools.partial(_combine_kernel, tm=tm),
        grid=(n // tm,),
        in_specs=[
            smem(), smem(),
            pl.BlockSpec(memory_space=pl.ANY),
            pl.BlockSpec((tm, LANES), lambda i: (i, 0)),
            pl.BlockSpec((tm, d), lambda i: (i, 0)),
            pl.BlockSpec((1, 1, d), lambda i: (i // tiles_per_seq, 0, 0)),
            pl.BlockSpec((1, d), lambda i: (0, 0)),
        ],
        out_specs=pl.BlockSpec((tm, d), lambda i: (i, 0)),
        out_shape=jax.ShapeDtypeStruct((n, d), F32),
        scratch_shapes=[pltpu.VMEM((2, tm * SLAB, LANES), F32), pltpu.SemaphoreType.DMA],
        compiler_params=_cparams(("arbitrary",)),
        name="moe_combine",
    )(pos1, pos2, os_, info, x, gate, ln_final)


def _moe(x, lnw, shift, scale, gate, router_w, router_b, ln_final, wg, wu, wd, seq,
         tm=512, tf=1792):
    n, d = x.shape
    ne = router_w.shape[1]
    tm = min(tm, seq)
    hs, info, totals = _router(x, lnw, shift, scale, router_w, router_b, seq, tm)

    counts = totals[0, :ne].astype(jnp.int32)
    padded = (counts + tm - 1) // tm * tm
    ends = jnp.cumsum(padded)
    starts = ends - padded
    experts = jnp.arange(ne, dtype=jnp.int32)
    start_of = lambda e: jnp.sum(jnp.where(e[:, None] == experts, starts, 0), axis=1)
    e1, e2 = info[:, 0].astype(jnp.int32), info[:, 1].astype(jnp.int32)
    pos1 = start_of(e1) + info[:, 4].astype(jnp.int32)
    pos2 = start_of(e2) + info[:, 5].astype(jnp.int32)
    n_tiles = (2 * n) // tm + ne
    tile_ids = jnp.arange(n_tiles, dtype=jnp.int32)
    tile_expert = jnp.minimum(jnp.sum(tile_ids[:, None] >= ends // tm, axis=1), ne - 1).astype(jnp.int32)
    n_used = (ends[-1:] // tm).astype(jnp.int32)

    xs = _dispatch(pos1, pos2, hs, n_tiles * tm, tm)
    os_ = _experts(tile_expert, n_used, xs, wg, wu, wd, tm, tf)
    return _combine(pos1, pos2, os_, info, x, gate, ln_final, seq, tm)


def kernel(x, c, rel_bias, ada_w, ada_b, ln_mix, ln_ffn, ln_final, even_w_in, even_w_out, lru_conv_w, lru_conv_b, lru_gate_a_w, lru_gate_a_b, lru_gate_x_w, lru_gate_x_b, lru_lambda, ffn_w_gate, ffn_w_up, ffn_w_down, odd_w_in, odd_w_out, diff_lambda_q1, diff_lambda_k1, diff_lambda_q2, diff_lambda_k2, diff_subln, router_w, router_b, moe_w_gate, moe_w_up, moe_w_down):
    batch, seq, d = x.shape
    n = batch * seq
    xf = x.reshape(n, d)
    mod = _adaln(c, ada_w, ada_b)
    part = lambda layer, j: mod[layer, :, j * d:(j + 1) * d].reshape(batch, 1, d)
    row = lambda v: v.reshape(1, -1)

    sh1, sc1, g1, sh2, sc2, g2 = (part(0, j) for j in range(6))
    qkv_w = 3 * SB_WIDTH
    qkv, xg = _inproj(xf, row(ln_mix[0]), sh1, sc1, even_w_in[0].astype(BF16),
                      ((0, qkv_w), (qkv_w, qkv_w + 2 * LRU_WIDTH)), (BF16, F32), seq)
    ya = _sb_attention(qkv, batch, seq)
    yb = _rg_lru(xg, lru_conv_w[0], lru_conv_b[0], lru_gate_a_w[0], lru_gate_a_b[0],
                 lru_gate_x_w[0], lru_gate_x_b[0], lru_lambda[0], batch, seq)
    xf = _outproj(xf, g1, even_w_out[0].astype(BF16), (ya, yb), seq)
    xf = _ffn(xf, row(ln_ffn[0]), sh2, sc2, g2, ffn_w_gate[0].astype(BF16),
              ffn_w_up[0].astype(BF16), ffn_w_down[0].astype(BF16), seq)

    sh1, sc1, g1, sh2, sc2, g2 = (part(1, j) for j in range(6))
    lambda_init = 0.8 - 0.6 * math.exp(-0.3 * 1)
    (qkv,) = _inproj(xf, row(ln_mix[1]), sh1, sc1, odd_w_in[0].astype(BF16),
                     ((0, 2 * DIFF_QK + DIFF_V),), (BF16,), seq)
    yo = _diff_attention(qkv, rel_bias, diff_lambda_q1[0], diff_lambda_k1[0], diff_lambda_q2[0],
                         diff_lambda_k2[0], diff_subln[0], lambda_init, batch, seq)
    xf = _outproj(xf, g1, odd_w_out[0].astype(BF16), (yo,), seq)
    out = _moe(xf, row(ln_ffn[1]), sh2, sc2, g2, router_w[0], router_b[0], row(ln_final),
               moe_w_gate[0].astype(BF16), moe_w_up[0].astype(BF16), moe_w_down[0].astype(BF16), seq)
    return out.reshape(batch, seq, d)
```

```python
import functools
import math

import jax
import jax.numpy as jnp
from jax import lax
from jax.experimental import pallas as pl
from jax.experimental.pallas import tpu as pltpu

F32 = jnp.float32
BF16 = jnp.bfloat16

D_MODEL = 1024
HEAD_DIM = 64
LANES = 128
SLAB = D_MODEL // LANES
SB_HEADS = 8
SB_WIDTH = SB_HEADS * HEAD_DIM
LRU_WIDTH = 512
LRU_BLOCKS = 8
CONV_WIDTH = 4
LRU_C = 8.0
DIFF_HEADS = 8
DIFF_QK = 2 * DIFF_HEADS * HEAD_DIM
DIFF_V = DIFF_HEADS * 2 * HEAD_DIM
N_BUCKETS = 32
MAX_EXACT = N_BUCKETS // 2
MAX_DISTANCE = 128
D_FF_DENSE = 2816
N_EXPERTS = 8
D_FF_EXPERT = 3584
EPS = 1e-6
NEG_BIG = -1e30
UNDERFLOW_LOG = 110.0

VMEM_LIMIT = 56 * 1024 * 1024


def _cparams(sem):
    return pltpu.CompilerParams(dimension_semantics=sem, vmem_limit_bytes=VMEM_LIMIT)


def _dot(a, b):
    return jnp.dot(a, b, preferred_element_type=F32)


def _dot_nt(a, b):
    return lax.dot_general(a, b, (((1,), (1,)), ((), ())), preferred_element_type=F32)


def _split_bf16(x):
    hi = x.astype(BF16)
    lo = (x - hi.astype(F32)).astype(BF16)
    return hi, lo


def _sigmoid(x):
    return 1.0 / (1.0 + jnp.exp(-x))


def _norm_mod(x, lnw, shift, scale):
    ms = jnp.mean(x * x, axis=-1, keepdims=True)
    n = x * lax.rsqrt(ms + EPS)
    return (n * lnw) * (1.0 + scale) + shift


def _adaln_kernel(c_ref, w_ref, b_ref, o_ref):
    c = c_ref[...]
    cond = c * _sigmoid(c)
    ch, cl = _split_bf16(cond)
    wh, wl = _split_bf16(w_ref[0])
    o_ref[0] = _dot(ch, wh) + _dot(ch, wl) + _dot(cl, wh) + b_ref[0]


def _adaln(c, ada_w, ada_b):
    depth, d, n6 = ada_w.shape
    b = c.shape[0]
    tn = 1536
    return pl.pallas_call(
        _adaln_kernel,
        grid=(depth, n6 // tn),
        in_specs=[
            pl.BlockSpec((b, d), lambda l, j: (0, 0)),
            pl.BlockSpec((1, d, tn), lambda l, j: (l, 0, j)),
            pl.BlockSpec((1, 1, tn), lambda l, j: (l, 0, j)),
        ],
        out_specs=pl.BlockSpec((1, b, tn), lambda l, j: (l, 0, j)),
        out_shape=jax.ShapeDtypeStruct((depth, b, n6), F32),
        compiler_params=_cparams(("arbitrary", "arbitrary")),
        name="adaln",
    )(c, ada_w, ada_b.reshape(depth, 1, n6))


def _inproj_kernel(x_ref, lnw_ref, sh_ref, sc_ref, w_ref, *o_refs, splits):
    h = _norm_mod(x_ref[...], lnw_ref[...], sh_ref[0], sc_ref[0]).astype(BF16)
    for o_ref, (lo, hi) in zip(o_refs, splits):
        o_ref[...] = _dot(h, w_ref[:, lo:hi]).astype(o_ref.dtype)


def _inproj(x, lnw, shift, scale, w, splits, dtypes, seq, tm=512):
    n, d = x.shape
    tiles_per_seq = seq // tm
    bidx = lambda i: (i // tiles_per_seq, 0, 0)
    return pl.pallas_call(
        functools.partial(_inproj_kernel, splits=splits),
        grid=(n // tm,),
        in_specs=[
            pl.BlockSpec((tm, d), lambda i: (i, 0)),
            pl.BlockSpec((1, d), lambda i: (0, 0)),
            pl.BlockSpec((1, 1, d), bidx),
            pl.BlockSpec((1, 1, d), bidx),
            pl.BlockSpec(w.shape, lambda i: (0, 0)),
        ],
        out_specs=[pl.BlockSpec((tm, hi - lo), lambda i: (i, 0)) for lo, hi in splits],
        out_shape=[jax.ShapeDtypeStruct((n, hi - lo), dt) for (lo, hi), dt in zip(splits, dtypes)],
        compiler_params=_cparams(("arbitrary",)),
        name="inproj",
    )(x, lnw, shift, scale, w)


def _sb_kernel(q_ref, k_ref, v_ref, o_ref, acc_ref, c_ref, *, tq, tk, hp):
    qi = pl.program_id(2)
    ratio = tq // tk
    n_tiles = (qi + 1) * ratio
    nc = tk // LANES
    nh = 2 * hp
    lane = lax.broadcasted_iota(jnp.int32, (tq, LANES), 1)
    blk = lambda p: slice(p * LANES, (p + 1) * LANES)
    qs = []
    for p in range(hp):
        q = q_ref[:, blk(p)] * (HEAD_DIM ** -0.5)
        zero = jnp.zeros_like(q)
        qs += [jnp.where(lane < HEAD_DIM, q, zero), jnp.where(lane >= HEAD_DIM, q, zero)]
    rj = lax.broadcasted_iota(jnp.int32, (tk, tk), 0)
    cs = lax.broadcasted_iota(jnp.int32, (tk, tk), 1)
    neg_upper = jnp.where(rj > cs, -1.0, 0.0).astype(BF16)

    acc_ref[...] = jnp.zeros_like(acc_ref)
    c_ref[...] = jnp.zeros_like(c_ref)

    def tile(j, masked, row_lo=0):
        start = pl.multiple_of(j * tk, tk)
        rows = tq - row_lo
        if masked:
            q_pos = qi * tq + row_lo + lax.broadcasted_iota(jnp.int32, (rows, tk), 0)
            k_pos = j * tk + lax.broadcasted_iota(jnp.int32, (rows, tk), 1)
            mask = k_pos < q_pos
        for hh in range(nh):
            ks = k_ref[pl.ds(start, tk), blk(hh // 2)]
            vs = v_ref[pl.ds(start, tk), blk(hh // 2)]
            z = _dot_nt(qs[hh][row_lo:], ks)
            sp = jnp.maximum(z, 0.0) + jnp.log(1.0 + jnp.exp(-jnp.abs(z)))
            if masked:
                sp = jnp.where(mask, sp, 0.0)
            hi, lo = _split_bf16(sp)
            later = _dot(hi, neg_upper) + _dot(lo, neg_upper)
            c = c_ref[hh, row_lo:]
            w = jnp.concatenate(
                [jnp.exp((z - sp)[:, i * LANES:(i + 1) * LANES]
                         + later[:, i * LANES:(i + 1) * LANES] - c) for i in range(nc)], axis=1)
            if masked:
                w = jnp.where(mask, w, 0.0)
            acc_ref[hh, row_lo:] += _dot(w.astype(BF16), vs)
            c_ref[hh, row_lo:] = c + jnp.broadcast_to(jnp.sum(sp, axis=1, keepdims=True),
                                                      (rows, LANES))

    for d in range(ratio):
        tile(n_tiles - 1 - d, True, row_lo=(ratio - 1 - d) * tk)

    def cond(state):
        n, c_min = state
        return (n < n_tiles - ratio) & (c_min < UNDERFLOW_LOG)

    def body(state):
        n, _ = state
        tile(n_tiles - ratio - 1 - n, False)
        return n + 1, jnp.min(c_ref[...])

    lax.while_loop(cond, body, (jnp.int32(0), jnp.min(c_ref[...])))

    for p in range(hp):
        o_ref[:, blk(p)] = jnp.where(lane < HEAD_DIM, acc_ref[2 * p],
                                     acc_ref[2 * p + 1]).astype(o_ref.dtype)


def _sb_attention(qkv, batch, seq, tq=512, tk=256, hp=1):
    tq = min(tq, seq)
    tk = min(tk, tq)
    n = qkv.shape[0]
    width = hp * LANES
    groups = SB_WIDTH // width
    nq = seq // tq
    return pl.pallas_call(
        functools.partial(_sb_kernel, tq=tq, tk=tk, hp=hp),
        grid=(batch, groups, nq),
        in_specs=[
            pl.BlockSpec((tq, width), lambda b, p, i: (b * nq + i, p)),
            pl.BlockSpec((seq, width), lambda b, p, i: (b, groups + p)),
            pl.BlockSpec((seq, width), lambda b, p, i: (b, 2 * groups + p)),
        ],
        out_specs=pl.BlockSpec((tq, width), lambda b, p, i: (b * nq + i, p)),
        out_shape=jax.ShapeDtypeStruct((n, SB_WIDTH), BF16),
        scratch_shapes=[
            pltpu.VMEM((2 * hp, tq, LANES), F32),
            pltpu.VMEM((2 * hp, tq, LANES), F32),
        ],
        compiler_params=_cparams(("arbitrary", "arbitrary", "arbitrary")),
        name="sb_attention",
    )(qkv, qkv, qkv)


def _lru_kernel(xg_ref, cw_ref, cb_ref, wg_ref, bg_ref, lam_ref, o_ref,
                ext_ref, a_ref, u_ref, h_ref, carry_ref, *, tt):
    c = LRU_WIDTH
    halo = 8

    @pl.when(pl.program_id(1) == 0)
    def _():
        ext_ref[0:halo, :] = jnp.zeros((halo, c), F32)
        carry_ref[...] = jnp.zeros_like(carry_ref)

    xb = xg_ref[:, 0:c]
    gb = xg_ref[:, c:2 * c]
    ext_ref[halo:halo + tt, :] = xb
    xc = cb_ref[...] + cw_ref[CONV_WIDTH - 1:CONV_WIDTH, :] * xb
    for i in range(CONV_WIDTH - 1):
        off = halo - (CONV_WIDTH - 1) + i
        xc = xc + cw_ref[i:i + 1, :] * ext_ref[off:off + tt, :]
    ext_ref[0:halo, :] = xb[tt - halo:tt, :]

    gates = _dot(xc.astype(BF16), wg_ref[...]) + bg_ref[...]
    r = _sigmoid(gates[:, 0:c])
    ig = _sigmoid(gates[:, c:2 * c])
    lam = lam_ref[...]
    log_sig_lam = jnp.minimum(lam, 0.0) - jnp.log(1.0 + jnp.exp(-jnp.abs(lam)))
    a = jnp.exp(LRU_C * r * log_sig_lam)
    u = jnp.sqrt(1.0 - a * a) * (ig * xc)

    r8 = lax.broadcasted_iota(jnp.int32, (tt, c), 0) & 7
    for s in (1, 2, 4):
        a_prev = pltpu.roll(a, s, axis=0)
        u_prev = pltpu.roll(u, s, axis=0)
        valid = r8 >= s
        u = jnp.where(valid, a * u_prev + u, u)
        a = jnp.where(valid, a * a_prev, a)
    a_ref[...] = a
    u_ref[...] = u

    def body(g, carry):
        off = pl.multiple_of(g * 8, 8)
        h = a_ref[pl.ds(off, 8), :] * carry + u_ref[pl.ds(off, 8), :]
        h_ref[pl.ds(off, 8), :] = h
        return jnp.broadcast_to(h[7:8, :], (8, c))

    carry_ref[...] = lax.fori_loop(0, tt // 8, body, carry_ref[...])

    gelu = 0.5 * gb * (1.0 + jnp.tanh(math.sqrt(2.0 / math.pi) * (gb + 0.044715 * (gb * gb * gb))))
    o_ref[...] = (h_ref[...] * gelu).astype(o_ref.dtype)


def _block_diag(w):
    g, bi, bo = w.shape
    eye = jnp.eye(g, dtype=w.dtype)
    return (w[:, :, None, :] * eye[:, None, :, None]).reshape(g * bi, g * bo)


def _rg_lru(xg, conv_w, conv_b, ga_w, ga_b, gx_w, gx_b, lam, batch, seq, tt=512):
    tt = min(tt, seq)
    n = xg.shape[0]
    c = LRU_WIDTH
    nt = seq // tt
    wg = jnp.concatenate([_block_diag(ga_w), _block_diag(gx_w)], axis=1).astype(BF16)
    bg = jnp.concatenate([ga_b, gx_b]).reshape(1, 2 * c)
    const = lambda b, t: (0, 0)
    return pl.pallas_call(
        functools.partial(_lru_kernel, tt=tt),
        grid=(batch, nt),
        in_specs=[
            pl.BlockSpec((tt, 2 * c), lambda b, t: (b * nt + t, 0)),
            pl.BlockSpec((CONV_WIDTH, c), const),
            pl.BlockSpec((1, c), const),
            pl.BlockSpec((c, 2 * c), const),
            pl.BlockSpec((1, 2 * c), const),
            pl.BlockSpec((1, c), const),
        ],
        out_specs=pl.BlockSpec((tt, c), lambda b, t: (b * nt + t, 0)),
        out_shape=jax.ShapeDtypeStruct((n, c), BF16),
        scratch_shapes=[
            pltpu.VMEM((tt + 8, c), F32),
            pltpu.VMEM((tt, c), F32),
            pltpu.VMEM((tt, c), F32),
            pltpu.VMEM((tt, c), F32),
            pltpu.VMEM((8, c), F32),
        ],
        compiler_params=_cparams(("arbitrary", "arbitrary")),
        name="rg_lru",
    )(xg, conv_w, conv_b.reshape(1, c), wg, bg, lam.reshape(1, c))


def _outproj_kernel(*refs, widths):
    x_ref, g_ref, w_ref = refs[0], refs[1], refs[2]
    y_refs = refs[3:3 + len(widths)]
    o_ref = refs[3 + len(widths)]
    mix = None
    off = 0
    for y_ref, wd in zip(y_refs, widths):
        part = _dot(y_ref[...], w_ref[off:off + wd, :])
        mix = part if mix is None else mix + part
        off += wd
    o_ref[...] = x_ref[...] + g_ref[0] * mix


def _outproj(x, gate, w, ys, seq, tm=512):
    n, d = x.shape
    tiles_per_seq = seq // tm
    widths = tuple(y.shape[1] for y in ys)
    return pl.pallas_call(
        functools.partial(_outproj_kernel, widths=widths),
        grid=(n // tm,),
        in_specs=[
            pl.BlockSpec((tm, d), lambda i: (i, 0)),
            pl.BlockSpec((1, 1, d), lambda i: (i // tiles_per_seq, 0, 0)),
            pl.BlockSpec(w.shape, lambda i: (0, 0)),
        ] + [pl.BlockSpec((tm, wd), lambda i: (i, 0)) for wd in widths],
        out_specs=pl.BlockSpec((tm, d), lambda i: (i, 0)),
        out_shape=jax.ShapeDtypeStruct((n, d), F32),
        compiler_params=_cparams(("arbitrary",)),
        name="outproj",
    )(x, gate, w, *ys)


def _ffn_kernel(x_ref, lnw_ref, sh_ref, sc_ref, g_ref, wg_ref, wu_ref, wd_ref, o_ref,
                h_ref, acc_ref):
    k = pl.program_id(1)

    @pl.when(k == 0)
    def _():
        h_ref[...] = _norm_mod(x_ref[...], lnw_ref[...], sh_ref[0], sc_ref[0]).astype(BF16)
        acc_ref[...] = jnp.zeros_like(acc_ref)

    h = h_ref[...]
    gt = _dot(h, wg_ref[...])
    up = _dot(h, wu_ref[...])
    act = (gt * _sigmoid(gt) * up).astype(BF16)
    acc_ref[...] += _dot(act, wd_ref[...])

    @pl.when(k == pl.num_programs(1) - 1)
    def _():
        o_ref[...] = x_ref[...] + g_ref[0] * acc_ref[...]


def _ffn(x, lnw, shift, scale, gate, wg, wu, wd, seq, tm=512, tf=None):
    n, d = x.shape
    f = wg.shape[1]
    tf = f if tf is None else tf
    weight_mode = pl.Buffered(1) if tf == f else None
    tiles_per_seq = seq // tm
    bidx = lambda i, k: (i // tiles_per_seq, 0, 0)
    return pl.pallas_call(
        _ffn_kernel,
        grid=(n // tm, f // tf),
        in_specs=[
            pl.BlockSpec((tm, d), lambda i, k: (i, 0)),
            pl.BlockSpec((1, d), lambda i, k: (0, 0)),
            pl.BlockSpec((1, 1, d), bidx),
            pl.BlockSpec((1, 1, d), bidx),
            pl.BlockSpec((1, 1, d), bidx),
            pl.BlockSpec((d, tf), lambda i, k: (0, k), pipeline_mode=weight_mode),
            pl.BlockSpec((d, tf), lambda i, k: (0, k), pipeline_mode=weight_mode),
            pl.BlockSpec((tf, d), lambda i, k: (k, 0), pipeline_mode=weight_mode),
        ],
        out_specs=pl.BlockSpec((tm, d), lambda i, k: (i, 0)),
        out_shape=jax.ShapeDtypeStruct((n, d), F32),
        scratch_shapes=[pltpu.VMEM((tm, d), BF16), pltpu.VMEM((tm, d), F32)],
        compiler_params=_cparams(("arbitrary", "arbitrary")),
        name="ffn_dense",
    )(x, lnw, shift, scale, gate, wg, wu, wd)


def _bias_kernel(rb_ref, o_ref, *, tq, tk, n_near):
    h = pl.program_id(0)
    t = pl.program_id(1)
    row = lax.broadcasted_iota(jnp.int32, (tq, tk), 0)
    col = lax.broadcasted_iota(jnp.int32, (tq, tk), 1)
    rel = row - col + (n_near - 1 - t) * tk - (tq - tk)
    nn = jnp.maximum(rel, 0)
    nf = jnp.maximum(nn, 1).astype(F32)
    large = MAX_EXACT + (jnp.log(nf / MAX_EXACT) / math.log(MAX_DISTANCE / MAX_EXACT)
                         * (N_BUCKETS - MAX_EXACT)).astype(jnp.int32)
    large = jnp.minimum(large, N_BUCKETS - 1)
    bucket = jnp.where(nn < MAX_EXACT, nn, large)
    bias = jnp.zeros((tq, tk), F32)
    for b in range(N_BUCKETS):
        bias = jnp.where(bucket == b, rb_ref[b, h], bias)
    o_ref[0, 0] = jnp.where(rel >= 0, bias, NEG_BIG)


def _near_bias(rel_bias, tq, tk, n_near):
    return pl.pallas_call(
        functools.partial(_bias_kernel, tq=tq, tk=tk, n_near=n_near),
        grid=(DIFF_HEADS, n_near),
        in_specs=[pl.BlockSpec(memory_space=pltpu.SMEM)],
        out_specs=pl.BlockSpec((1, 1, tq, tk), lambda h, t: (h, t, 0, 0)),
        out_shape=jax.ShapeDtypeStruct((DIFF_HEADS, n_near, tq, tk), F32),
        compiler_params=_cparams(("arbitrary", "arbitrary")),
        name="t5_bias_tiles",
    )(rel_bias)


def _diff_kernel(rb_ref, q_ref, k_ref, v_ref, nb_ref, lq1_ref, lk1_ref, lq2_ref, lk2_ref,
                 sub_ref, o_ref, m_ref, acc_ref, vext_ref, z_ref, *, tq, tk, n_near,
                 lambda_init):
    h = pl.program_id(1)
    qi = pl.program_id(2)
    ratio = tq // tk
    lane = lax.broadcasted_iota(jnp.int32, (tq, LANES), 1)
    q = q_ref[...] * (HEAD_DIM ** -0.5)
    zero = jnp.zeros_like(q)
    qs = (jnp.where(lane < HEAD_DIM, q, zero), jnp.where(lane >= HEAD_DIM, q, zero))

    @pl.when(qi == 0)
    def _():
        vext_ref[:, 0:LANES] = v_ref[...]
        vext_ref[:, LANES:2 * LANES] = jnp.ones(v_ref.shape, BF16)

    m_ref[...] = jnp.full_like(m_ref, NEG_BIG)
    acc_ref[...] = jnp.zeros_like(acc_ref)
    nc = tk // LANES

    def scores(j, slot):
        start = pl.multiple_of(j * tk, tk)
        ks = k_ref[pl.ds(start, tk), :]
        for mm in range(2):
            z_ref[slot, mm] = _dot_nt(qs[mm], ks)

    def softmax_pv(j, slot, bias, bias_is_scalar):
        start = pl.multiple_of(j * tk, tk)
        vs = vext_ref[pl.ds(start, tk), :]
        for mm in range(2):
            zc = [z_ref[slot, mm, :, c * LANES:(c + 1) * LANES] for c in range(nc)]
            if not bias_is_scalar:
                zc = [zc[c] + bias[:, c * LANES:(c + 1) * LANES] for c in range(nc)]
            part = functools.reduce(jnp.maximum, zc)
            rm = jnp.broadcast_to(jnp.max(part, axis=1, keepdims=True), (tq, LANES))
            m_old = m_ref[mm]
            if bias_is_scalar:
                m_new = jnp.maximum(m_old, rm + bias)
                shift = m_new - bias
            else:
                m_new = jnp.maximum(m_old, rm)
                shift = m_new
            alpha = jnp.exp(m_old - m_new)
            p = jnp.concatenate([jnp.exp(z - shift) for z in zc], axis=1).astype(BF16)
            pv = _dot(p, vs)
            for c in range(2):
                sl = slice(c * LANES, (c + 1) * LANES)
                acc_ref[mm, :, sl] = alpha * acc_ref[mm, :, sl] + pv[:, sl]
            m_ref[mm] = m_new

    n_tiles = (qi + 1) * ratio
    n_far = jnp.maximum(n_tiles - n_near, 0)
    far_bias = rb_ref[N_BUCKETS - 1, h]
    scores(0, 0)

    def body(pair, carry):
        j = 2 * pair
        softmax_pv(j, 0, far_bias, True)
        scores(j + 1, 1)
        softmax_pv(j + 1, 1, far_bias, True)
        scores(j + 2, 0)
        return carry

    lax.fori_loop(0, jnp.right_shift(n_far, 1), body, 0)

    @pl.when((n_far & 1) == 1)
    def _():
        softmax_pv(n_far - 1, 0, far_bias, True)
        scores(n_far, 1)

    for t in range(n_near):
        j = n_tiles - n_near + t

        @pl.when(j >= 0)
        def _():
            slot = j & 1
            softmax_pv(j, slot, nb_ref[0, t], False)
            if t < n_near - 1:
                scores(j + 1, 1 - slot)

    lam = (jnp.exp(jnp.sum(lq1_ref[...] * lk1_ref[...], axis=1, keepdims=True))
           - jnp.exp(jnp.sum(lq2_ref[...] * lk2_ref[...], axis=1, keepdims=True))
           + lambda_init)
    o = (acc_ref[0, :, 0:LANES] / acc_ref[0, :, LANES:2 * LANES]
         - lam * (acc_ref[1, :, 0:LANES] / acc_ref[1, :, LANES:2 * LANES]))
    ms = jnp.mean(o * o, axis=-1, keepdims=True)
    o = o * lax.rsqrt(ms + EPS) * sub_ref[...] * (1.0 - lambda_init)
    o_ref[...] = o.astype(o_ref.dtype)


def _diff_attention(qkv, rel_bias, lq1, lk1, lq2, lk2, subln, lambda_init, batch, seq,
                    tq=512, tk=512):
    tq = min(tq, seq)
    tk = min(tk, tq)
    assert tk >= MAX_DISTANCE
    n = qkv.shape[0]
    nq = seq // tq
    nh = DIFF_HEADS
    n_near = tq // tk + 1
    near = _near_bias(rel_bias, tq, tk, n_near)
    vec = lambda a: a.reshape(1, -1)
    cvec = pl.BlockSpec((1, HEAD_DIM), lambda b, h, i: (0, 0))
    return pl.pallas_call(
        functools.partial(_diff_kernel, tq=tq, tk=tk, n_near=n_near, lambda_init=lambda_init),
        grid=(batch, nh, nq),
        in_specs=[
            pl.BlockSpec(memory_space=pltpu.SMEM),
            pl.BlockSpec((tq, LANES), lambda b, h, i: (b * nq + i, h)),
            pl.BlockSpec((seq, LANES), lambda b, h, i: (b, nh + h)),
            pl.BlockSpec((seq, LANES), lambda b, h, i: (b, 2 * nh + h)),
            pl.BlockSpec((1, n_near, tq, tk), lambda b, h, i: (h, 0, 0, 0)),
            cvec, cvec, cvec, cvec,
            pl.BlockSpec((1, LANES), lambda b, h, i: (0, 0)),
        ],
        out_specs=pl.BlockSpec((tq, LANES), lambda b, h, i: (b * nq + i, h)),
        out_shape=jax.ShapeDtypeStruct((n, DIFF_V), BF16),
        scratch_shapes=[
            pltpu.VMEM((2, tq, LANES), F32),
            pltpu.VMEM((2, tq, 2 * LANES), F32),
            pltpu.VMEM((seq, 2 * LANES), BF16),
            pltpu.VMEM((2, 2, tq, tk), F32),
        ],
        compiler_params=_cparams(("arbitrary", "arbitrary", "arbitrary")),
        name="diff_attention",
    )(rel_bias, qkv, qkv, qkv, near, vec(lq1), vec(lk1), vec(lq2), vec(lk2), vec(subln))


def _to_slabs(ref, value):
    rows = value.shape[0]
    for j in range(SLAB):
        ref[pl.ds(j, rows, stride=SLAB), :] = value[:, j * LANES:(j + 1) * LANES]


def _from_slabs(ref, rows):
    return jnp.concatenate([ref[pl.ds(j, rows, stride=SLAB), :] for j in range(SLAB)], axis=1)


def _router_kernel(x_ref, lnw_ref, sh_ref, sc_ref, rw_ref, rb_ref,
                   hs_ref, info_ref, totals_ref, run_ref, *, tm):
    @pl.when(pl.program_id(0) == 0)
    def _():
        run_ref[...] = jnp.zeros_like(run_ref)

    lane = lax.broadcasted_iota(jnp.int32, (tm, LANES), 1)
    h = _norm_mod(x_ref[...], lnw_ref[...], sh_ref[0], sc_ref[0])
    _to_slabs(hs_ref, h)
    hh, hl = _split_bf16(h)
    wh, wl = _split_bf16(rw_ref[...])
    logits = _dot(hh, wh) + _dot(hl, wh) + _dot(hh, wl) + rb_ref[...]
    neg_inf = jnp.float32(-jnp.inf)
    lanef = lane.astype(F32)
    logits = jnp.where(lane < N_EXPERTS, logits, neg_inf)
    m1 = jnp.max(logits, axis=1, keepdims=True)
    i1 = jnp.min(jnp.where(logits == m1, lanef, float(LANES)), axis=1, keepdims=True)
    rest = jnp.where(lanef == i1, neg_inf, logits)
    m2 = jnp.max(rest, axis=1, keepdims=True)
    i2 = jnp.min(jnp.where(rest == m2, lanef, float(LANES)), axis=1, keepdims=True)
    w1 = 1.0 / (1.0 + jnp.exp(m2 - m1))
    w2 = 1.0 - w1
    picked_f = jnp.where((lanef == i1) | (lanef == i2), 1.0, 0.0)
    ri = lax.broadcasted_iota(jnp.int32, (tm, tm), 0)
    ci = lax.broadcasted_iota(jnp.int32, (tm, tm), 1)
    earlier = jnp.where(ci < ri, 1.0, 0.0).astype(BF16)
    rank = _dot(earlier, picked_f.astype(BF16)) + run_ref[...]
    pick = lambda idx: jnp.sum(jnp.where(lanef == idx, rank, 0.0), axis=1, keepdims=True)
    r1, r2 = pick(i1), pick(i2)
    cols = (i1, i2, w1, w2, r1, r2)
    info = jnp.zeros((tm, LANES), F32)
    for c, v in enumerate(cols):
        info = jnp.where(lane == c, v, info)
    info_ref[...] = info
    run_ref[...] += jnp.sum(picked_f, axis=0, keepdims=True)
    totals_ref[...] = run_ref[...]


def _dispatch_kernel(p1_ref, p2_ref, hs_ref, zero_ref, xs_ref, sem, *, tm):
    del zero_ref

    def row_copy(t, p_ref):
        src = pl.multiple_of(t * SLAB, SLAB)
        dst = pl.multiple_of(p_ref[t] * SLAB, SLAB)
        return pltpu.make_async_copy(hs_ref.at[pl.ds(src, SLAB)], xs_ref.at[pl.ds(dst, SLAB)], sem)

    def issue(t, carry):
        row_copy(t, p1_ref).start(priority=0)
        row_copy(t, p2_ref).start(priority=1)
        return carry

    lax.fori_loop(0, tm, issue, 0, unroll=8)

    def drain(t, carry):
        row_copy(t, p1_ref).wait()
        row_copy(t, p2_ref).wait()
        return carry

    lax.fori_loop(0, tm, drain, 0, unroll=8)


def _expert_kernel(te_ref, nu_ref, xs_ref, wg_ref, wu_ref, wd_ref, os_ref, xc_ref, acc_ref, *, tm):
    j = pl.program_id(0)
    k = pl.program_id(1)

    @pl.when(j < nu_ref[0])
    def _():
        @pl.when(k == 0)
        def _():
            xc_ref[...] = _from_slabs(xs_ref, tm).astype(BF16)
            acc_ref[...] = jnp.zeros_like(acc_ref)

        xc = xc_ref[...]
        gt = _dot(xc, wg_ref[0])
        up = _dot(xc, wu_ref[0])
        act = (gt * _sigmoid(gt) * up).astype(BF16)
        acc_ref[...] += _dot(act, wd_ref[0])

        @pl.when(k == pl.num_programs(1) - 1)
        def _():
            _to_slabs(os_ref, acc_ref[...])


def _combine_kernel(p1_ref, p2_ref, os_ref, info_ref, x_ref, g_ref, lnf_ref, o_ref,
                    buf_ref, sem, *, tm):
    def row_copy(t, p_ref, slot):
        src = pl.multiple_of(p_ref[t] * SLAB, SLAB)
        dst = pl.multiple_of(t * SLAB, SLAB)
        return pltpu.make_async_copy(os_ref.at[pl.ds(src, SLAB)],
                                     buf_ref.at[slot, pl.ds(dst, SLAB)], sem)

    def issue(t, carry):
        row_copy(t, p1_ref, 0).start(priority=0)
        row_copy(t, p2_ref, 1).start(priority=1)
        return carry

    lax.fori_loop(0, tm, issue, 0, unroll=8)

    def drain(t, carry):
        row_copy(t, p1_ref, 0).wait()
        row_copy(t, p2_ref, 1).wait()
        return carry

    lax.fori_loop(0, tm, drain, 0, unroll=8)

    info = info_ref[...]
    w1 = info[:, 2:3]
    w2 = info[:, 3:4]
    y = w1 * _from_slabs(buf_ref.at[0], tm) + w2 * _from_slabs(buf_ref.at[1], tm)
    xo = x_ref[...] + g_ref[0] * y
    ms = jnp.mean(xo * xo, axis=-1, keepdims=True)
    o_ref[...] = xo * lax.rsqrt(ms + EPS) * lnf_ref[...]


def _router(x, lnw, shift, scale, router_w, router_b, seq, tm):
    n, d = x.shape
    ne = router_w.shape[1]
    nt = n // tm
    tiles_per_seq = seq // tm
    rw = jnp.zeros((d, LANES), F32).at[:, :ne].set(router_w)
    rb = jnp.zeros((1, LANES), F32).at[0, :ne].set(router_b)
    bidx = lambda i: (i // tiles_per_seq, 0, 0)
    const = lambda i: (0, 0)
    return pl.pallas_call(
        functools.partial(_router_kernel, tm=tm),
        grid=(nt,),
        in_specs=[
            pl.BlockSpec((tm, d), lambda i: (i, 0)),
            pl.BlockSpec((1, d), const),
            pl.BlockSpec((1, 1, d), bidx),
            pl.BlockSpec((1, 1, d), bidx),
            pl.BlockSpec((d, LANES), const),
            pl.BlockSpec((1, LANES), const),
        ],
        out_specs=[
            pl.BlockSpec((tm * SLAB, LANES), lambda i: (i, 0)),
            pl.BlockSpec((tm, LANES), lambda i: (i, 0)),
            pl.BlockSpec((1, LANES), lambda i: (0, 0)),
        ],
        out_shape=[
            jax.ShapeDtypeStruct((n * SLAB, LANES), F32),
            jax.ShapeDtypeStruct((n, LANES), F32),
            jax.ShapeDtypeStruct((1, LANES), F32),
        ],
        scratch_shapes=[pltpu.VMEM((1, LANES), F32)],
        compiler_params=_cparams(("arbitrary",)),
        name="moe_router",
    )(x, lnw, shift, scale, rw, rb)


def _dispatch(pos1, pos2, hs, n_rows, tm):
    n = pos1.shape[0]
    smem = lambda: pl.BlockSpec((tm,), lambda i: (i,), memory_space=pltpu.SMEM)
    anywhere = lambda: pl.BlockSpec(memory_space=pl.ANY)
    return pl.pallas_call(
        functools.partial(_dispatch_kernel, tm=tm),
        grid=(n // tm,),
        in_specs=[smem(), smem(), pl.BlockSpec((tm * SLAB, LANES), lambda i: (i, 0)), anywhere()],
        out_specs=anywhere(),
        out_shape=jax.ShapeDtypeStruct((n_rows * SLAB, LANES), F32),
        scratch_shapes=[pltpu.SemaphoreType.DMA],
        input_output_aliases={3: 0},
        compiler_params=_cparams(("arbitrary",)),
        name="moe_dispatch",
    )(pos1, pos2, hs, jnp.zeros((n_rows * SLAB, LANES), F32))


def _experts(tile_expert, n_used, xs, wg, wu, wd, tm, tf):
    ne, d, f = wg.shape
    n_tiles = xs.shape[0] // (tm * SLAB)
    tf = f if tf is None else tf
    weight_mode = pl.Buffered(1) if tf == f else None
    nk = f // tf
    rows = lambda j, k, te, nu: (jnp.minimum(j, nu[0] - 1), 0)
    chunk = lambda j, k, nu: jnp.where(j < nu[0], k, nk - 1)
    grid_spec = pltpu.PrefetchScalarGridSpec(
        num_scalar_prefetch=2,
        grid=(n_tiles, nk),
        in_specs=[
            pl.BlockSpec((tm * SLAB, LANES), rows),
            pl.BlockSpec((1, d, tf), lambda j, k, te, nu: (te[j], 0, chunk(j, k, nu)),
                         pipeline_mode=weight_mode),
            pl.BlockSpec((1, d, tf), lambda j, k, te, nu: (te[j], 0, chunk(j, k, nu)),
                         pipeline_mode=weight_mode),
            pl.BlockSpec((1, tf, d), lambda j, k, te, nu: (te[j], chunk(j, k, nu), 0),
                         pipeline_mode=weight_mode),
        ],
        out_specs=pl.BlockSpec((tm * SLAB, LANES), rows),
        scratch_shapes=[pltpu.VMEM((tm, d), BF16), pltpu.VMEM((tm, d), F32)],
    )
    return pl.pallas_call(
        functools.partial(_expert_kernel, tm=tm),
        grid_spec=grid_spec,
        out_shape=jax.ShapeDtypeStruct(xs.shape, F32),
        input_output_aliases={2: 0},
        compiler_params=_cparams(("arbitrary", "arbitrary")),
        name="moe_experts",
    )(tile_expert, n_used, xs, wg, wu, wd)


def _combine(pos1, pos2, os_, info, x, gate, ln_final, seq, tm):
    n, d = x.shape
    tiles_per_seq = seq // tm
    smem = lambda: pl.BlockSpec((tm,), lambda i: (i,), memory_space=pltpu.SMEM)
    return pl.pallas_call(
        functools.partial(_combine_kernel, tm=tm),
        grid=(n // tm,),
        in_specs=[
            smem(), smem(),
            pl.BlockSpec(memory_space=pl.ANY),
            pl.BlockSpec((tm, LANES), lambda i: (i, 0)),
            pl.BlockSpec((tm, d), lambda i: (i, 0)),
            pl.BlockSpec((1, 1, d), lambda i: (i // tiles_per_seq, 0, 0)),
            pl.BlockSpec((1, d), lambda i: (0, 0)),
        ],
        out_specs=pl.BlockSpec((tm, d), lambda i: (i, 0)),
        out_shape=jax.ShapeDtypeStruct((n, d), F32),
        scratch_shapes=[pltpu.VMEM((2, tm * SLAB, LANES), F32), pltpu.SemaphoreType.DMA],
        compiler_params=_cparams(("arbitrary",)),
        name="moe_combine",
    )(pos1, pos2, os_, info, x, gate, ln_final)


def _moe(x, lnw, shift, scale, gate, router_w, router_b, ln_final, wg, wu, wd, seq,
         tm=512, tf=None):
    n, d = x.shape
    ne = router_w.shape[1]
    tm = min(tm, seq)
    hs, info, totals = _router(x, lnw, shift, scale, router_w, router_b, seq, tm)

    counts = totals[0, :ne].astype(jnp.int32)
    padded = (counts + tm - 1) // tm * tm
    ends = jnp.cumsum(padded)
    starts = ends - padded
    experts = jnp.arange(ne, dtype=jnp.int32)
    start_of = lambda e: jnp.sum(jnp.where(e[:, None] == experts, starts, 0), axis=1)
    e1, e2 = info[:, 0].astype(jnp.int32), info[:, 1].astype(jnp.int32)
    pos1 = start_of(e1) + info[:, 4].astype(jnp.int32)
    pos2 = start_of(e2) + info[:, 5].astype(jnp.int32)
    n_tiles = (2 * n) // tm + ne
    tile_ids = jnp.arange(n_tiles, dtype=jnp.int32)
    tile_expert = jnp.minimum(jnp.sum(tile_ids[:, None] >= ends // tm, axis=1), ne - 1).astype(jnp.int32)
    n_used = (ends[-1:] // tm).astype(jnp.int32)

    xs = _dispatch(pos1, pos2, hs, n_tiles * tm, tm)
    os_ = _experts(tile_expert, n_used, xs, wg, wu, wd, tm, tf)
    return _combine(pos1, pos2, os_, info, x, gate, ln_final, seq, tm)


def kernel(x, c, rel_bias, ada_w, ada_b, ln_mix, ln_ffn, ln_final, even_w_in, even_w_out, lru_conv_w, lru_conv_b, lru_gate_a_w, lru_gate_a_b, lru_gate_x_w, lru_gate_x_b, lru_lambda, ffn_w_gate, ffn_w_up, ffn_w_down, odd_w_in, odd_w_out, diff_lambda_q1, diff_lambda_k1, diff_lambda_q2, diff_lambda_k2, diff_subln, router_w, router_b, moe_w_gate, moe_w_up, moe_w_down):
    batch, seq, d = x.shape
    n = batch * seq
    xf = x.reshape(n, d)
    mod = _adaln(c, ada_w, ada_b)
    part = lambda layer, j: mod[layer, :, j * d:(j + 1) * d].reshape(batch, 1, d)
    row = lambda v: v.reshape(1, -1)

    sh1, sc1, g1, sh2, sc2, g2 = (part(0, j) for j in range(6))
    qkv_w = 3 * SB_WIDTH
    qkv, xg = _inproj(xf, row(ln_mix[0]), sh1, sc1, even_w_in[0].astype(BF16),
                      ((0, qkv_w), (qkv_w, qkv_w + 2 * LRU_WIDTH)), (BF16, F32), seq)
    ya = _sb_attention(qkv, batch, seq)
    yb = _rg_lru(xg, lru_conv_w[0], lru_conv_b[0], lru_gate_a_w[0], lru_gate_a_b[0],
                 lru_gate_x_w[0], lru_gate_x_b[0], lru_lambda[0], batch, seq)
    xf = _outproj(xf, g1, even_w_out[0].astype(BF16), (ya, yb), seq)
    xf = _ffn(xf, row(ln_ffn[0]), sh2, sc2, g2, ffn_w_gate[0].astype(BF16),
              ffn_w_up[0].astype(BF16), ffn_w_down[0].astype(BF16), seq)

    sh1, sc1, g1, sh2, sc2, g2 = (part(1, j) for j in range(6))
    lambda_init = 0.8 - 0.6 * math.exp(-0.3 * 1)
    (qkv,) = _inproj(xf, row(ln_mix[1]), sh1, sc1, odd_w_in[0].astype(BF16),
                     ((0, 2 * DIFF_QK + DIFF_V),), (BF16,), seq)
    yo = _diff_attention(qkv, rel_bias, diff_lambda_q1[0], diff_lambda_k1[0], diff_lambda_q2[0],
                         diff_lambda_k2[0], diff_subln[0], lambda_init, batch, seq)
    xf = _outproj(xf, g1, odd_w_out[0].astype(BF16), (yo,), seq)
    out = _moe(xf, row(ln_ffn[1]), sh2, sc2, g2, router_w[0], router_b[0], row(ln_final),
               moe_w_gate[0].astype(BF16), moe_w_up[0].astype(BF16), moe_w_down[0].astype(BF16), seq)
    return out.reshape(batch, seq, d)
```

```python
import functools
import math

import jax
import jax.numpy as jnp
from jax import lax
from jax.experimental import pallas as pl
from jax.experimental.pallas import tpu as pltpu

F32 = jnp.float32
BF16 = jnp.bfloat16

D_MODEL = 1024
HEAD_DIM = 64
LANES = 128
SLAB = D_MODEL // LANES
SB_HEADS = 8
SB_WIDTH = SB_HEADS * HEAD_DIM
LRU_WIDTH = 512
LRU_BLOCKS = 8
CONV_WIDTH = 4
LRU_C = 8.0
DIFF_HEADS = 8
DIFF_QK = 2 * DIFF_HEADS * HEAD_DIM
DIFF_V = DIFF_HEADS * 2 * HEAD_DIM
N_BUCKETS = 32
MAX_EXACT = N_BUCKETS // 2
MAX_DISTANCE = 128
D_FF_DENSE = 2816
N_EXPERTS = 8
D_FF_EXPERT = 3584
EPS = 1e-6
NEG_BIG = -1e30
UNDERFLOW_LOG = 110.0

VMEM_LIMIT = 56 * 1024 * 1024


def _cparams(sem):
    return pltpu.CompilerParams(dimension_semantics=sem, vmem_limit_bytes=VMEM_LIMIT)


def _dot(a, b):
    return jnp.dot(a, b, preferred_element_type=F32)


def _dot_nt(a, b):
    return lax.dot_general(a, b, (((1,), (1,)), ((), ())), preferred_element_type=F32)


def _split_bf16(x):
    hi = x.astype(BF16)
    lo = (x - hi.astype(F32)).astype(BF16)
    return hi, lo


def _sigmoid(x):
    return 1.0 / (1.0 + jnp.exp(-x))


def _norm_mod(x, lnw, shift, scale):
    ms = jnp.mean(x * x, axis=-1, keepdims=True)
    n = x * lax.rsqrt(ms + EPS)
    return (n * lnw) * (1.0 + scale) + shift


def _adaln_kernel(c_ref, w_ref, b_ref, o_ref):
    c = c_ref[...]
    cond = c * _sigmoid(c)
    ch, cl = _split_bf16(cond)
    wh, wl = _split_bf16(w_ref[0])
    o_ref[0] = _dot(ch, wh) + _dot(ch, wl) + _dot(cl, wh) + b_ref[0]


def _adaln(c, ada_w, ada_b):
    depth, d, n6 = ada_w.shape
    b = c.shape[0]
    tn = 1536
    return pl.pallas_call(
        _adaln_kernel,
        grid=(depth, n6 // tn),
        in_specs=[
            pl.BlockSpec((b, d), lambda l, j: (0, 0)),
            pl.BlockSpec((1, d, tn), lambda l, j: (l, 0, j)),
            pl.BlockSpec((1, 1, tn), lambda l, j: (l, 0, j)),
        ],
        out_specs=pl.BlockSpec((1, b, tn), lambda l, j: (l, 0, j)),
        out_shape=jax.ShapeDtypeStruct((depth, b, n6), F32),
        compiler_params=_cparams(("arbitrary", "arbitrary")),
        name="adaln",
    )(c, ada_w, ada_b.reshape(depth, 1, n6))


def _inproj_kernel(x_ref, lnw_ref, sh_ref, sc_ref, w_ref, *o_refs, splits):
    h = _norm_mod(x_ref[...], lnw_ref[...], sh_ref[0], sc_ref[0]).astype(BF16)
    for o_ref, (lo, hi) in zip(o_refs, splits):
        o_ref[...] = _dot(h, w_ref[:, lo:hi]).astype(o_ref.dtype)


def _inproj(x, lnw, shift, scale, w, splits, dtypes, seq, tm=512):
    n, d = x.shape
    tiles_per_seq = seq // tm
    bidx = lambda i: (i // tiles_per_seq, 0, 0)
    return pl.pallas_call(
        functools.partial(_inproj_kernel, splits=splits),
        grid=(n // tm,),
        in_specs=[
            pl.BlockSpec((tm, d), lambda i: (i, 0)),
            pl.BlockSpec((1, d), lambda i: (0, 0)),
            pl.BlockSpec((1, 1, d), bidx),
            pl.BlockSpec((1, 1, d), bidx),
            pl.BlockSpec(w.shape, lambda i: (0, 0)),
        ],
        out_specs=[pl.BlockSpec((tm, hi - lo), lambda i: (i, 0)) for lo, hi in splits],
        out_shape=[jax.ShapeDtypeStruct((n, hi - lo), dt) for (lo, hi), dt in zip(splits, dtypes)],
        compiler_params=_cparams(("arbitrary",)),
        name="inproj",
    )(x, lnw, shift, scale, w)


def _sb_kernel(q_ref, k_ref, v_ref, o_ref, acc_ref, c_ref, *, tq, tk, hp):
    qi = pl.program_id(2)
    ratio = tq // tk
    n_tiles = (qi + 1) * ratio
    nc = tk // LANES
    nh = 2 * hp
    lane = lax.broadcasted_iota(jnp.int32, (tq, LANES), 1)
    blk = lambda p: slice(p * LANES, (p + 1) * LANES)
    qs = []
    for p in range(hp):
        q = q_ref[:, blk(p)] * (HEAD_DIM ** -0.5)
        zero = jnp.zeros_like(q)
        qs += [jnp.where(lane < HEAD_DIM, q, zero), jnp.where(lane >= HEAD_DIM, q, zero)]
    rj = lax.broadcasted_iota(jnp.int32, (tk, tk), 0)
    cs = lax.broadcasted_iota(jnp.int32, (tk, tk), 1)
    neg_upper = jnp.where(rj > cs, -1.0, 0.0).astype(BF16)

    acc_ref[...] = jnp.zeros_like(acc_ref)
    c_ref[...] = jnp.zeros_like(c_ref)

    def tile(j, masked, row_lo=0):
        start = pl.multiple_of(j * tk, tk)
        rows = tq - row_lo
        if masked:
            q_pos = qi * tq + row_lo + lax.broadcasted_iota(jnp.int32, (rows, tk), 0)
            k_pos = j * tk + lax.broadcasted_iota(jnp.int32, (rows, tk), 1)
            mask = k_pos < q_pos
        for hh in range(nh):
            ks = k_ref[pl.ds(start, tk), blk(hh // 2)]
            vs = v_ref[pl.ds(start, tk), blk(hh // 2)]
            z = _dot_nt(qs[hh][row_lo:], ks)
            sp = jnp.maximum(z, 0.0) + jnp.log(1.0 + jnp.exp(-jnp.abs(z)))
            if masked:
                sp = jnp.where(mask, sp, 0.0)
            hi, lo = _split_bf16(sp)
            later = _dot(hi, neg_upper) + _dot(lo, neg_upper)
            c = c_ref[hh, row_lo:]
            w = jnp.concatenate(
                [jnp.exp((z - sp)[:, i * LANES:(i + 1) * LANES]
                         + later[:, i * LANES:(i + 1) * LANES] - c) for i in range(nc)], axis=1)
            if masked:
                w = jnp.where(mask, w, 0.0)
            acc_ref[hh, row_lo:] += _dot(w.astype(BF16), vs)
            c_ref[hh, row_lo:] = c + jnp.broadcast_to(jnp.sum(sp, axis=1, keepdims=True),
                                                      (rows, LANES))

    for d in range(ratio):
        tile(n_tiles - 1 - d, True, row_lo=(ratio - 1 - d) * tk)

    def cond(state):
        n, c_min = state
        return (n < n_tiles - ratio) & (c_min < UNDERFLOW_LOG)

    def body(state):
        n, _ = state
        tile(n_tiles - ratio - 1 - n, False)
        return n + 1, jnp.min(c_ref[...])

    lax.while_loop(cond, body, (jnp.int32(0), jnp.min(c_ref[...])))

    for p in range(hp):
        o_ref[:, blk(p)] = jnp.where(lane < HEAD_DIM, acc_ref[2 * p],
                                     acc_ref[2 * p + 1]).astype(o_ref.dtype)


def _sb_attention(qkv, batch, seq, tq=512, tk=256, hp=1):
    tq = min(tq, seq)
    tk = min(tk, tq)
    n = qkv.shape[0]
    width = hp * LANES
    groups = SB_WIDTH // width
    nq = seq // tq
    return pl.pallas_call(
        functools.partial(_sb_kernel, tq=tq, tk=tk, hp=hp),
        grid=(batch, groups, nq),
        in_specs=[
            pl.BlockSpec((tq, width), lambda b, p, i: (b * nq + i, p)),
            pl.BlockSpec((seq, width), lambda b, p, i: (b, groups + p)),
            pl.BlockSpec((seq, width), lambda b, p, i: (b, 2 * groups + p)),
        ],
        out_specs=pl.BlockSpec((tq, width), lambda b, p, i: (b * nq + i, p)),
        out_shape=jax.ShapeDtypeStruct((n, SB_WIDTH), BF16),
        scratch_shapes=[
            pltpu.VMEM((2 * hp, tq, LANES), F32),
            pltpu.VMEM((2 * hp, tq, LANES), F32),
        ],
        compiler_params=_cparams(("arbitrary", "arbitrary", "arbitrary")),
        name="sb_attention",
    )(qkv, qkv, qkv)


def _lru_kernel(xg_ref, cw_ref, cb_ref, wg_ref, bg_ref, lam_ref, o_ref,
                ext_ref, a_ref, u_ref, h_ref, carry_ref, *, tt):
    c = LRU_WIDTH
    halo = 8

    @pl.when(pl.program_id(1) == 0)
    def _():
        ext_ref[0:halo, :] = jnp.zeros((halo, c), F32)
        carry_ref[...] = jnp.zeros_like(carry_ref)

    xb = xg_ref[:, 0:c]
    gb = xg_ref[:, c:2 * c]
    ext_ref[halo:halo + tt, :] = xb
    xc = cb_ref[...] + cw_ref[CONV_WIDTH - 1:CONV_WIDTH, :] * xb
    for i in range(CONV_WIDTH - 1):
        off = halo - (CONV_WIDTH - 1) + i
        xc = xc + cw_ref[i:i + 1, :] * ext_ref[off:off + tt, :]
    ext_ref[0:halo, :] = xb[tt - halo:tt, :]

    gates = _dot(xc.astype(BF16), wg_ref[...]) + bg_ref[...]
    r = _sigmoid(gates[:, 0:c])
    ig = _sigmoid(gates[:, c:2 * c])
    lam = lam_ref[...]
    log_sig_lam = jnp.minimum(lam, 0.0) - jnp.log(1.0 + jnp.exp(-jnp.abs(lam)))
    a = jnp.exp(LRU_C * r * log_sig_lam)
    u = jnp.sqrt(1.0 - a * a) * (ig * xc)

    a = a.reshape(tt // 8, 8, c)
    u = u.reshape(tt // 8, 8, c)
    r8 = lax.broadcasted_iota(jnp.int32, (tt // 8, 8, c), 1)
    for s in (1, 2, 4):
        a_prev = pltpu.roll(a, s, axis=1)
        u_prev = pltpu.roll(u, s, axis=1)
        valid = r8 >= s
        u = jnp.where(valid, a * u_prev + u, u)
        a = jnp.where(valid, a * a_prev, a)
    a_ref[...] = a.reshape(tt, c)
    u_ref[...] = u.reshape(tt, c)

    def body(g, carry):
        off = pl.multiple_of(g * 8, 8)
        h = a_ref[pl.ds(off, 8), :] * carry + u_ref[pl.ds(off, 8), :]
        h_ref[pl.ds(off, 8), :] = h
        return jnp.broadcast_to(h[7:8, :], (8, c))

    carry_ref[...] = lax.fori_loop(0, tt // 8, body, carry_ref[...])

    gelu = 0.5 * gb * (1.0 + jnp.tanh(math.sqrt(2.0 / math.pi) * (gb + 0.044715 * (gb * gb * gb))))
    o_ref[...] = (h_ref[...] * gelu).astype(o_ref.dtype)


def _block_diag(w):
    g, bi, bo = w.shape
    eye = jnp.eye(g, dtype=w.dtype)
    return (w[:, :, None, :] * eye[:, None, :, None]).reshape(g * bi, g * bo)


def _rg_lru(xg, conv_w, conv_b, ga_w, ga_b, gx_w, gx_b, lam, batch, seq, tt=512):
    tt = min(tt, seq)
    n = xg.shape[0]
    c = LRU_WIDTH
    nt = seq // tt
    wg = jnp.concatenate([_block_diag(ga_w), _block_diag(gx_w)], axis=1).astype(BF16)
    bg = jnp.concatenate([ga_b, gx_b]).reshape(1, 2 * c)
    const = lambda b, t: (0, 0)
    return pl.pallas_call(
        functools.partial(_lru_kernel, tt=tt),
        grid=(batch, nt),
        in_specs=[
            pl.BlockSpec((tt, 2 * c), lambda b, t: (b * nt + t, 0)),
            pl.BlockSpec((CONV_WIDTH, c), const),
            pl.BlockSpec((1, c), const),
            pl.BlockSpec((c, 2 * c), const),
            pl.BlockSpec((1, 2 * c), const),
            pl.BlockSpec((1, c), const),
        ],
        out_specs=pl.BlockSpec((tt, c), lambda b, t: (b * nt + t, 0)),
        out_shape=jax.ShapeDtypeStruct((n, c), BF16),
        scratch_shapes=[
            pltpu.VMEM((tt + 8, c), F32),
            pltpu.VMEM((tt, c), F32),
            pltpu.VMEM((tt, c), F32),
            pltpu.VMEM((tt, c), F32),
            pltpu.VMEM((8, c), F32),
        ],
        compiler_params=_cparams(("arbitrary", "arbitrary")),
        name="rg_lru",
    )(xg, conv_w, conv_b.reshape(1, c), wg, bg, lam.reshape(1, c))


def _outproj_kernel(*refs, widths):
    x_ref, g_ref, w_ref = refs[0], refs[1], refs[2]
    y_refs = refs[3:3 + len(widths)]
    o_ref = refs[3 + len(widths)]
    mix = None
    off = 0
    for y_ref, wd in zip(y_refs, widths):
        part = _dot(y_ref[...], w_ref[off:off + wd, :])
        mix = part if mix is None else mix + part
        off += wd
    o_ref[...] = x_ref[...] + g_ref[0] * mix


def _outproj(x, gate, w, ys, seq, tm=512):
    n, d = x.shape
    tiles_per_seq = seq // tm
    widths = tuple(y.shape[1] for y in ys)
    return pl.pallas_call(
        functools.partial(_outproj_kernel, widths=widths),
        grid=(n // tm,),
        in_specs=[
            pl.BlockSpec((tm, d), lambda i: (i, 0)),
            pl.BlockSpec((1, 1, d), lambda i: (i // tiles_per_seq, 0, 0)),
            pl.BlockSpec(w.shape, lambda i: (0, 0)),
        ] + [pl.BlockSpec((tm, wd), lambda i: (i, 0)) for wd in widths],
        out_specs=pl.BlockSpec((tm, d), lambda i: (i, 0)),
        out_shape=jax.ShapeDtypeStruct((n, d), F32),
        compiler_params=_cparams(("arbitrary",)),
        name="outproj",
    )(x, gate, w, *ys)


def _mix_ffn_kernel(x_ref, ya_ref, yb_ref, wo_ref, g1_ref, lnw_ref, sh_ref, sc_ref, g2_ref,
                    wg_ref, wu_ref, wd_ref, o_ref):
    wa = ya_ref.shape[1]
    mix = _dot(ya_ref[...], wo_ref[0:wa, :]) + _dot(yb_ref[...], wo_ref[wa:, :])
    x1 = x_ref[...] + g1_ref[0] * mix
    h = _norm_mod(x1, lnw_ref[...], sh_ref[0], sc_ref[0]).astype(BF16)
    gt = _dot(h, wg_ref[...])
    up = _dot(h, wu_ref[...])
    act = (gt * _sigmoid(gt) * up).astype(BF16)
    o_ref[...] = x1 + g2_ref[0] * _dot(act, wd_ref[...])


def _mix_ffn(x, ya, yb, wo, g1, lnw, shift, scale, g2, wg, wu, wd, seq, tm=512):
    n, d = x.shape
    tiles_per_seq = seq // tm
    row_tile = lambda width: pl.BlockSpec((tm, width), lambda i: (i, 0))
    per_batch = pl.BlockSpec((1, 1, d), lambda i: (i // tiles_per_seq, 0, 0))
    resident = lambda w: pl.BlockSpec(w.shape, lambda i: (0, 0), pipeline_mode=pl.Buffered(1))
    return pl.pallas_call(
        _mix_ffn_kernel,
        grid=(n // tm,),
        in_specs=[
            row_tile(d), row_tile(ya.shape[1]), row_tile(yb.shape[1]), resident(wo), per_batch,
            pl.BlockSpec((1, d), lambda i: (0, 0)), per_batch, per_batch, per_batch,
            resident(wg), resident(wu), resident(wd),
        ],
        out_specs=row_tile(d),
        out_shape=jax.ShapeDtypeStruct((n, d), F32),
        compiler_params=_cparams(("arbitrary",)),
        name="mix_ffn_dense",
    )(x, ya, yb, wo, g1, lnw, shift, scale, g2, wg, wu, wd)


def _bias_kernel(rb_ref, o_ref, *, tq, tk, n_near):
    h = pl.program_id(0)
    t = pl.program_id(1)
    row = lax.broadcasted_iota(jnp.int32, (tq, tk), 0)
    col = lax.broadcasted_iota(jnp.int32, (tq, tk), 1)
    rel = row - col + (n_near - 1 - t) * tk - (tq - tk)
    nn = jnp.maximum(rel, 0)
    nf = jnp.maximum(nn, 1).astype(F32)
    large = MAX_EXACT + (jnp.log(nf / MAX_EXACT) / math.log(MAX_DISTANCE / MAX_EXACT)
                         * (N_BUCKETS - MAX_EXACT)).astype(jnp.int32)
    large = jnp.minimum(large, N_BUCKETS - 1)
    bucket = jnp.where(nn < MAX_EXACT, nn, large)
    bias = jnp.zeros((tq, tk), F32)
    for b in range(N_BUCKETS):
        bias = jnp.where(bucket == b, rb_ref[b, h], bias)
    o_ref[0, 0] = jnp.where(rel >= 0, bias, NEG_BIG)


def _near_bias(rel_bias, tq, tk, n_near):
    return pl.pallas_call(
        functools.partial(_bias_kernel, tq=tq, tk=tk, n_near=n_near),
        grid=(DIFF_HEADS, n_near),
        in_specs=[pl.BlockSpec(memory_space=pltpu.SMEM)],
        out_specs=pl.BlockSpec((1, 1, tq, tk), lambda h, t: (h, t, 0, 0)),
        out_shape=jax.ShapeDtypeStruct((DIFF_HEADS, n_near, tq, tk), F32),
        compiler_params=_cparams(("arbitrary", "arbitrary")),
        name="t5_bias_tiles",
    )(rel_bias)


def _diff_kernel(rb_ref, q_ref, k_ref, v_ref, nb_ref, lq1_ref, lk1_ref, lq2_ref, lk2_ref,
                 sub_ref, o_ref, m_ref, acc_ref, vext_ref, z_ref, *, tq, tk, n_near,
                 lambda_init, hp):
    h0 = pl.program_id(1) * hp
    qi = pl.program_id(2)
    ratio = tq // tk
    lane = lax.broadcasted_iota(jnp.int32, (tq, LANES), 1)
    blk = lambda hd: slice(hd * LANES, (hd + 1) * LANES)
    qs = []
    for hd in range(hp):
        q = q_ref[:, blk(hd)] * (HEAD_DIM ** -0.5)
        zero = jnp.zeros_like(q)
        qs += [jnp.where(lane < HEAD_DIM, q, zero), jnp.where(lane >= HEAD_DIM, q, zero)]

    @pl.when(qi == 0)
    def _():
        for hd in range(hp):
            vext_ref[hd, :, 0:LANES] = v_ref[:, blk(hd)]
            vext_ref[hd, :, LANES:2 * LANES] = jnp.ones((v_ref.shape[0], LANES), BF16)

    m_ref[...] = jnp.full_like(m_ref, NEG_BIG)
    acc_ref[...] = jnp.zeros_like(acc_ref)
    nc = tk // LANES

    def scores(j, slot):
        start = pl.multiple_of(j * tk, tk)
        for ch in range(2 * hp):
            z_ref[slot, ch] = _dot_nt(qs[ch], k_ref[pl.ds(start, tk), blk(ch // 2)])

    def softmax_pv(j, slot, biases, bias_is_scalar):
        start = pl.multiple_of(j * tk, tk)
        for mm in range(2 * hp):
            bias = biases[mm // 2]
            vs = vext_ref[mm // 2, pl.ds(start, tk), :]
            zc = [z_ref[slot, mm, :, c * LANES:(c + 1) * LANES] for c in range(nc)]
            if not bias_is_scalar:
                zc = [zc[c] + bias[:, c * LANES:(c + 1) * LANES] for c in range(nc)]
            part = functools.reduce(jnp.maximum, zc)
            rm = jnp.broadcast_to(jnp.max(part, axis=1, keepdims=True), (tq, LANES))
            m_old = m_ref[mm]
            if bias_is_scalar:
                m_new = jnp.maximum(m_old, rm + bias)
                shift = m_new - bias
            else:
                m_new = jnp.maximum(m_old, rm)
                shift = m_new
            alpha = jnp.exp(m_old - m_new)
            p = jnp.concatenate([jnp.exp(z - shift) for z in zc], axis=1).astype(BF16)
            pv = _dot(p, vs)
            for c in range(2):
                sl = slice(c * LANES, (c + 1) * LANES)
                acc_ref[mm, :, sl] = alpha * acc_ref[mm, :, sl] + pv[:, sl]
            m_ref[mm] = m_new

    n_tiles = (qi + 1) * ratio
    n_far = jnp.maximum(n_tiles - n_near, 0)
    far_bias = [rb_ref[N_BUCKETS - 1, h0 + hd] for hd in range(hp)]
    scores(0, 0)

    def body(pair, carry):
        j = 2 * pair
        softmax_pv(j, 0, far_bias, True)
        scores(j + 1, 1)
        softmax_pv(j + 1, 1, far_bias, True)
        scores(j + 2, 0)
        return carry

    lax.fori_loop(0, jnp.right_shift(n_far, 1), body, 0)

    @pl.when((n_far & 1) == 1)
    def _():
        softmax_pv(n_far - 1, 0, far_bias, True)
        scores(n_far, 1)

    for t in range(n_near):
        j = n_tiles - n_near + t

        @pl.when(j >= 0)
        def _():
            slot = j & 1
            softmax_pv(j, slot, [nb_ref[hd, t] for hd in range(hp)], False)
            if t < n_near - 1:
                scores(j + 1, 1 - slot)

    lam = (jnp.exp(jnp.sum(lq1_ref[...] * lk1_ref[...], axis=1, keepdims=True))
           - jnp.exp(jnp.sum(lq2_ref[...] * lk2_ref[...], axis=1, keepdims=True))
           + lambda_init)
    for hd in range(hp):
        a1, a2 = 2 * hd, 2 * hd + 1
        o = (acc_ref[a1, :, 0:LANES] / acc_ref[a1, :, LANES:2 * LANES]
             - lam * (acc_ref[a2, :, 0:LANES] / acc_ref[a2, :, LANES:2 * LANES]))
        ms = jnp.mean(o * o, axis=-1, keepdims=True)
        o = o * lax.rsqrt(ms + EPS) * sub_ref[...] * (1.0 - lambda_init)
        o_ref[:, blk(hd)] = o.astype(o_ref.dtype)


def _diff_attention(qkv, rel_bias, lq1, lk1, lq2, lk2, subln, lambda_init, batch, seq,
                    tq=512, tk=512, hp=2):
    tq = min(tq, seq)
    tk = min(tk, tq)
    assert tk >= MAX_DISTANCE
    n = qkv.shape[0]
    nq = seq // tq
    ng = DIFF_HEADS // hp
    width = hp * LANES
    n_near = tq // tk + 1
    near = _near_bias(rel_bias, tq, tk, n_near)
    vec = lambda a: a.reshape(1, -1)
    cvec = pl.BlockSpec((1, HEAD_DIM), lambda b, h, i: (0, 0))
    return pl.pallas_call(
        functools.partial(_diff_kernel, tq=tq, tk=tk, n_near=n_near, lambda_init=lambda_init,
                          hp=hp),
        grid=(batch, ng, nq),
        in_specs=[
            pl.BlockSpec(memory_space=pltpu.SMEM),
            pl.BlockSpec((tq, width), lambda b, h, i: (b * nq + i, h)),
            pl.BlockSpec((seq, width), lambda b, h, i: (b, ng + h)),
            pl.BlockSpec((seq, width), lambda b, h, i: (b, 2 * ng + h)),
            pl.BlockSpec((hp, n_near, tq, tk), lambda b, h, i: (h, 0, 0, 0)),
            cvec, cvec, cvec, cvec,
            pl.BlockSpec((1, LANES), lambda b, h, i: (0, 0)),
        ],
        out_specs=pl.BlockSpec((tq, width), lambda b, h, i: (b * nq + i, h)),
        out_shape=jax.ShapeDtypeStruct((n, DIFF_V), BF16),
        scratch_shapes=[
            pltpu.VMEM((2 * hp, tq, LANES), F32),
            pltpu.VMEM((2 * hp, tq, 2 * LANES), F32),
            pltpu.VMEM((hp, seq, 2 * LANES), BF16),
            pltpu.VMEM((2, 2 * hp, tq, tk), F32),
        ],
        compiler_params=_cparams(("arbitrary", "arbitrary", "arbitrary")),
        name="diff_attention",
    )(rel_bias, qkv, qkv, qkv, near, vec(lq1), vec(lk1), vec(lq2), vec(lk2), vec(subln))


def _to_slabs(ref, value):
    rows = value.shape[0]
    for j in range(SLAB):
        ref[pl.ds(j, rows, stride=SLAB), :] = value[:, j * LANES:(j + 1) * LANES]


def _from_slabs(ref, rows):
    return jnp.concatenate([ref[pl.ds(j, rows, stride=SLAB), :] for j in range(SLAB)], axis=1)


def _router_kernel(x_ref, lnw_ref, sh_ref, sc_ref, rw_ref, rb_ref,
                   hs_ref, info_ref, totals_ref, run_ref, *, tm):
    @pl.when(pl.program_id(0) == 0)
    def _():
        run_ref[...] = jnp.zeros_like(run_ref)

    lane = lax.broadcasted_iota(jnp.int32, (tm, LANES), 1)
    h = _norm_mod(x_ref[...], lnw_ref[...], sh_ref[0], sc_ref[0])
    _to_slabs(hs_ref, h)
    hh, hl = _split_bf16(h)
    wh, wl = _split_bf16(rw_ref[...])
    logits = _dot(hh, wh) + _dot(hl, wh) + _dot(hh, wl) + rb_ref[...]
    neg_inf = jnp.float32(-jnp.inf)
    lanef = lane.astype(F32)
    logits = jnp.where(lane < N_EXPERTS, logits, neg_inf)
    m1 = jnp.max(logits, axis=1, keepdims=True)
    i1 = jnp.min(jnp.where(logits == m1, lanef, float(LANES)), axis=1, keepdims=True)
    rest = jnp.where(lanef == i1, neg_inf, logits)
    m2 = jnp.max(rest, axis=1, keepdims=True)
    i2 = jnp.min(jnp.where(rest == m2, lanef, float(LANES)), axis=1, keepdims=True)
    w1 = 1.0 / (1.0 + jnp.exp(m2 - m1))
    w2 = 1.0 - w1
    picked_f = jnp.where((lanef == i1) | (lanef == i2), 1.0, 0.0)
    ri = lax.broadcasted_iota(jnp.int32, (tm, tm), 0)
    ci = lax.broadcasted_iota(jnp.int32, (tm, tm), 1)
    earlier = jnp.where(ci < ri, 1.0, 0.0).astype(BF16)
    rank = _dot(earlier, picked_f.astype(BF16)) + run_ref[...]
    pick = lambda idx: jnp.sum(jnp.where(lanef == idx, rank, 0.0), axis=1, keepdims=True)
    r1, r2 = pick(i1), pick(i2)
    cols = (i1, i2, w1, w2, r1, r2)
    info = jnp.zeros((tm, LANES), F32)
    for c, v in enumerate(cols):
        info = jnp.where(lane == c, v, info)
    info_ref[...] = info
    run_ref[...] += jnp.sum(picked_f, axis=0, keepdims=True)
    totals_ref[...] = run_ref[...]


def _dispatch_kernel(p1_ref, p2_ref, hs_ref, zero_ref, xs_ref, sem, *, tm):
    del zero_ref

    def row_copy(t, p_ref):
        src = pl.multiple_of(t * SLAB, SLAB)
        dst = pl.multiple_of(p_ref[t] * SLAB, SLAB)
        return pltpu.make_async_copy(hs_ref.at[pl.ds(src, SLAB)], xs_ref.at[pl.ds(dst, SLAB)], sem)

    def issue(t, carry):
        row_copy(t, p1_ref).start(priority=0)
        row_copy(t, p2_ref).start(priority=1)
        return carry

    lax.fori_loop(0, tm, issue, 0, unroll=8)

    def drain(t, carry):
        row_copy(t, p1_ref).wait()
        row_copy(t, p2_ref).wait()
        return carry

    lax.fori_loop(0, tm, drain, 0, unroll=8)


def _expert_kernel(te_ref, nu_ref, xs_ref, wg_ref, wu_ref, wd_ref, os_ref, xc_ref, acc_ref, *, tm):
    j = pl.program_id(0)
    k = pl.program_id(1)

    @pl.when(j < nu_ref[0])
    def _():
        @pl.when(k == 0)
        def _():
            xc_ref[...] = _from_slabs(xs_ref, tm).astype(BF16)
            acc_ref[...] = jnp.zeros_like(acc_ref)

        xc = xc_ref[...]
        gt = _dot(xc, wg_ref[0])
        up = _dot(xc, wu_ref[0])
        act = (gt * _sigmoid(gt) * up).astype(BF16)
        acc_ref[...] += _dot(act, wd_ref[0])

        @pl.when(k == pl.num_programs(1) - 1)
        def _():
            _to_slabs(os_ref, acc_ref[...])


def _combine_kernel(p1_ref, p2_ref, os_ref, info_ref, x_ref, g_ref, lnf_ref, o_ref,
                    buf_ref, sem, *, tm):
    def row_copy(t, p_ref, slot):
        src = pl.multiple_of(p_ref[t] * SLAB, SLAB)
        dst = pl.multiple_of(t * SLAB, SLAB)
        return pltpu.make_async_copy(os_ref.at[pl.ds(src, SLAB)],
                                     buf_ref.at[slot, pl.ds(dst, SLAB)], sem)

    def issue(t, carry):
        row_copy(t, p1_ref, 0).start(priority=0)
        row_copy(t, p2_ref, 1).start(priority=1)
        return carry

    lax.fori_loop(0, tm, issue, 0, unroll=8)

    def drain(t, carry):
        row_copy(t, p1_ref, 0).wait()
        row_copy(t, p2_ref, 1).wait()
        return carry

    lax.fori_loop(0, tm, drain, 0, unroll=8)

    info = info_ref[...]
    w1 = info[:, 2:3]
    w2 = info[:, 3:4]
    y = w1 * _from_slabs(buf_ref.at[0], tm) + w2 * _from_slabs(buf_ref.at[1], tm)
    xo = x_ref[...] + g_ref[0] * y
    ms = jnp.mean(xo * xo, axis=-1, keepdims=True)
    o_ref[...] = xo * lax.rsqrt(ms + EPS) * lnf_ref[...]


def _router(x, lnw, shift, scale, router_w, router_b, seq, tm):
    n, d = x.shape
    ne = router_w.shape[1]
    nt = n // tm
    tiles_per_seq = seq // tm
    rw = jnp.zeros((d, LANES), F32).at[:, :ne].set(router_w)
    rb = jnp.zeros((1, LANES), F32).at[0, :ne].set(router_b)
    bidx = lambda i: (i // tiles_per_seq, 0, 0)
    const = lambda i: (0, 0)
    return pl.pallas_call(
        functools.partial(_router_kernel, tm=tm),
        grid=(nt,),
        in_specs=[
            pl.BlockSpec((tm, d), lambda i: (i, 0)),
            pl.BlockSpec((1, d), const),
            pl.BlockSpec((1, 1, d), bidx),
            pl.BlockSpec((1, 1, d), bidx),
            pl.BlockSpec((d, LANES), const),
            pl.BlockSpec((1, LANES), const),
        ],
        out_specs=[
            pl.BlockSpec((tm * SLAB, LANES), lambda i: (i, 0)),
            pl.BlockSpec((tm, LANES), lambda i: (i, 0)),
            pl.BlockSpec((1, LANES), lambda i: (0, 0)),
        ],
        out_shape=[
            jax.ShapeDtypeStruct((n * SLAB, LANES), F32),
            jax.ShapeDtypeStruct((n, LANES), F32),
            jax.ShapeDtypeStruct((1, LANES), F32),
        ],
        scratch_shapes=[pltpu.VMEM((1, LANES), F32)],
        compiler_params=_cparams(("arbitrary",)),
        name="moe_router",
    )(x, lnw, shift, scale, rw, rb)


def _dispatch(pos1, pos2, hs, n_rows, tm):
    n = pos1.shape[0]
    smem = lambda: pl.BlockSpec((tm,), lambda i: (i,), memory_space=pltpu.SMEM)
    anywhere = lambda: pl.BlockSpec(memory_space=pl.ANY)
    return pl.pallas_call(
        functools.partial(_dispatch_kernel, tm=tm),
        grid=(n // tm,),
        in_specs=[smem(), smem(), pl.BlockSpec((tm * SLAB, LANES), lambda i: (i, 0)), anywhere()],
        out_specs=anywhere(),
        out_shape=jax.ShapeDtypeStruct((n_rows * SLAB, LANES), F32),
        scratch_shapes=[pltpu.SemaphoreType.DMA],
        input_output_aliases={3: 0},
        compiler_params=_cparams(("arbitrary",)),
        name="moe_dispatch",
    )(pos1, pos2, hs, jnp.zeros((n_rows * SLAB, LANES), F32))


def _experts(tile_expert, n_used, xs, wg, wu, wd, tm, tf):
    ne, d, f = wg.shape
    n_tiles = xs.shape[0] // (tm * SLAB)
    tf = f if tf is None else tf
    weight_mode = pl.Buffered(1) if tf == f else None
    nk = f // tf
    rows = lambda j, k, te, nu: (jnp.minimum(j, nu[0] - 1), 0)
    chunk = lambda j, k, nu: jnp.where(j < nu[0], k, nk - 1)
    grid_spec = pltpu.PrefetchScalarGridSpec(
        num_scalar_prefetch=2,
        grid=(n_tiles, nk),
        in_specs=[
            pl.BlockSpec((tm * SLAB, LANES), rows),
            pl.BlockSpec((1, d, tf), lambda j, k, te, nu: (te[j], 0, chunk(j, k, nu)),
                         pipeline_mode=weight_mode),
            pl.BlockSpec((1, d, tf), lambda j, k, te, nu: (te[j], 0, chunk(j, k, nu)),
                         pipeline_mode=weight_mode),
            pl.BlockSpec((1, tf, d), lambda j, k, te, nu: (te[j], chunk(j, k, nu), 0),
                         pipeline_mode=weight_mode),
        ],
        out_specs=pl.BlockSpec((tm * SLAB, LANES), rows),
        scratch_shapes=[pltpu.VMEM((tm, d), BF16), pltpu.VMEM((tm, d), F32)],
    )
    return pl.pallas_call(
        functools.partial(_expert_kernel, tm=tm),
        grid_spec=grid_spec,
        out_shape=jax.ShapeDtypeStruct(xs.shape, F32),
        input_output_aliases={2: 0},
        compiler_params=_cparams(("arbitrary", "arbitrary")),
        name="moe_experts",
    )(tile_expert, n_used, xs, wg, wu, wd)


def _combine(pos1, pos2, os_, info, x, gate, ln_final, seq, tm):
    n, d = x.shape
    tiles_per_seq = seq // tm
    smem = lambda: pl.BlockSpec((tm,), lambda i: (i,), memory_space=pltpu.SMEM)
    return pl.pallas_call(
        functools.partial(_combine_kernel, tm=tm),
        grid=(n // tm,),
        in_specs=[
            smem(), smem(),
            pl.BlockSpec(memory_space=pl.ANY),
            pl.BlockSpec((tm, LANES), lambda i: (i, 0)),
            pl.BlockSpec((tm, d), lambda i: (i, 0)),
            pl.BlockSpec((1, 1, d), lambda i: (i // tiles_per_seq, 0, 0)),
            pl.BlockSpec((1, d), lambda i: (0, 0)),
        ],
        out_specs=pl.BlockSpec((tm, d), lambda i: (i, 0)),
        out_shape=jax.ShapeDtypeStruct((n, d), F32),
        scratch_shapes=[pltpu.VMEM((2, tm * SLAB, LANES), F32), pltpu.SemaphoreType.DMA],
        compiler_params=_cparams(("arbitrary",)),
        name="moe_combine",
    )(pos1, pos2, os_, info, x, gate, ln_final)


def _moe(x, lnw, shift, scale, gate, router_w, router_b, ln_final, wg, wu, wd, seq,
         tm=512, tf=None):
    n, d = x.shape
    ne = router_w.shape[1]
    tm = min(tm, seq)
    hs, info, totals = _router(x, lnw, shift, scale, router_w, router_b, seq, tm)

    counts = totals[0, :ne].astype(jnp.int32)
    padded = (counts + tm - 1) // tm * tm
    ends = jnp.cumsum(padded)
    starts = ends - padded
    experts = jnp.arange(ne, dtype=jnp.int32)
    start_of = lambda e: jnp.sum(jnp.where(e[:, None] == experts, starts, 0), axis=1)
    e1, e2 = info[:, 0].astype(jnp.int32), info[:, 1].astype(jnp.int32)
    pos1 = start_of(e1) + info[:, 4].astype(jnp.int32)
    pos2 = start_of(e2) + info[:, 5].astype(jnp.int32)
    n_tiles = (2 * n) // tm + ne
    tile_ids = jnp.arange(n_tiles, dtype=jnp.int32)
    tile_expert = jnp.minimum(jnp.sum(tile_ids[:, None] >= ends // tm, axis=1), ne - 1).astype(jnp.int32)
    n_used = (ends[-1:] // tm).astype(jnp.int32)

    xs = _dispatch(pos1, pos2, hs, n_tiles * tm, tm)
    os_ = _experts(tile_expert, n_used, xs, wg, wu, wd, tm, tf)
    return _combine(pos1, pos2, os_, info, x, gate, ln_final, seq, tm)


def kernel(x, c, rel_bias, ada_w, ada_b, ln_mix, ln_ffn, ln_final, even_w_in, even_w_out, lru_conv_w, lru_conv_b, lru_gate_a_w, lru_gate_a_b, lru_gate_x_w, lru_gate_x_b, lru_lambda, ffn_w_gate, ffn_w_up, ffn_w_down, odd_w_in, odd_w_out, diff_lambda_q1, diff_lambda_k1, diff_lambda_q2, diff_lambda_k2, diff_subln, router_w, router_b, moe_w_gate, moe_w_up, moe_w_down):
    batch, seq, d = x.shape
    n = batch * seq
    xf = x.reshape(n, d)
    mod = _adaln(c, ada_w, ada_b)
    part = lambda layer, j: mod[layer, :, j * d:(j + 1) * d].reshape(batch, 1, d)
    row = lambda v: v.reshape(1, -1)

    sh1, sc1, g1, sh2, sc2, g2 = (part(0, j) for j in range(6))
    qkv_w = 3 * SB_WIDTH
    qkv, xg = _inproj(xf, row(ln_mix[0]), sh1, sc1, even_w_in[0].astype(BF16),
                      ((0, qkv_w), (qkv_w, qkv_w + 2 * LRU_WIDTH)), (BF16, F32), seq)
    ya = _sb_attention(qkv, batch, seq)
    yb = _rg_lru(xg, lru_conv_w[0], lru_conv_b[0], lru_gate_a_w[0], lru_gate_a_b[0],
                 lru_gate_x_w[0], lru_gate_x_b[0], lru_lambda[0], batch, seq)
    xf = _mix_ffn(xf, ya, yb, even_w_out[0].astype(BF16), g1, row(ln_ffn[0]), sh2, sc2, g2,
                  ffn_w_gate[0].astype(BF16), ffn_w_up[0].astype(BF16),
                  ffn_w_down[0].astype(BF16), seq)

    sh1, sc1, g1, sh2, sc2, g2 = (part(1, j) for j in range(6))
    lambda_init = 0.8 - 0.6 * math.exp(-0.3 * 1)
    (qkv,) = _inproj(xf, row(ln_mix[1]), sh1, sc1, odd_w_in[0].astype(BF16),
                     ((0, 2 * DIFF_QK + DIFF_V),), (BF16,), seq)
    yo = _diff_attention(qkv, rel_bias, diff_lambda_q1[0], diff_lambda_k1[0], diff_lambda_q2[0],
                         diff_lambda_k2[0], diff_subln[0], lambda_init, batch, seq)
    xf = _outproj(xf, g1, odd_w_out[0].astype(BF16), (yo,), seq)
    out = _moe(xf, row(ln_ffn[1]), sh2, sc2, g2, router_w[0], router_b[0], row(ln_final),
               moe_w_gate[0].astype(BF16), moe_w_up[0].astype(BF16), moe_w_down[0].astype(BF16), seq)
    return out.reshape(batch, seq, d)
```
